```python
import jax, jax.numpy as jnp
from jax import lax
import numpy as np

D_MODEL = 1024
BATCH = 2
SEQ = 8192
DEPTH = 4
DEC_BATCH = 16
DEC_SEQ = 64
PAST_LEN = 1024

CHUNK = 64
D_MIX = D_MODEL
D_CONV = D_MIX // 2
D_SGU = D_MIX - D_CONV
N_SGU_HEADS = 8
SGU_HEAD_DIM = D_SGU // N_SGU_HEADS
SGU_CHUNK = 128
CONV_WIDTH = 31
CONV_CTX = CONV_WIDTH - 1
D_FF = 2816
ALPHA = float((2 * DEPTH) ** 0.25)
BETA = float((8 * DEPTH) ** -0.25)
LN_EPS = 1e-5

kernel_name = "hybrid_conv_sgu_streaming_encoder_step"


def layer_norm(x, g, b):
    xf = x.astype(jnp.float32)
    mu = jnp.mean(xf, axis=-1, keepdims=True)
    var = jnp.mean(jnp.square(xf - mu), axis=-1, keepdims=True)
    y = (xf - mu) * lax.rsqrt(var + LN_EPS) * g.astype(jnp.float32) + b.astype(jnp.float32)
    return y.astype(x.dtype)


def swiglu(x, wg, wu, wd):
    return (jax.nn.silu(x @ wg) * (x @ wu)) @ wd


def depthwise_causal(buf, k, bias):
    c = buf.shape[-1]
    out = lax.conv_general_dilated(buf, k[:, None, :], window_strides=(1,), padding='VALID',
                                   dimension_numbers=('NWC', 'WIO', 'NWC'), feature_group_count=c)
    return out + bias


def sgu_mix(v, w_s, b_s):
    bsz, t, _ = v.shape
    L = min(t, SGU_CHUNK)
    vh = v.reshape(bsz, t // L, L, N_SGU_HEADS, SGU_HEAD_DIM)
    w = jnp.tril(w_s[:, :L, :L])
    out = jnp.einsum('hts,bnshd->bnthd', w, vh) + b_s[:, :L].T[None, None, :, :, None]
    return out.reshape(bsz, t, D_SGU)


def trunk_layer(x, conv_ctx, w_ffn1_gate, w_ffn1_up, w_ffn1_down, ln1_g, ln1_b, w_in, conv_k, conv_b,
                conv_ln_g, conv_ln_b, sgu_ln_g, sgu_ln_b, w_sgu, b_sgu, w_out, ln2_g, ln2_b,
                w_ffn2_gate, w_ffn2_up, w_ffn2_down, ln3_g, ln3_b):
    x = layer_norm(ALPHA * x + 0.5 * swiglu(x, w_ffn1_gate, w_ffn1_up, w_ffn1_down), ln1_g, ln1_b)
    h = x @ w_in
    a_val = h[..., :D_CONV]
    a_gate = h[..., D_CONV:2 * D_CONV]
    z = jax.nn.gelu(h[..., 2 * D_CONV:], approximate=False)
    a = a_val * jax.nn.sigmoid(a_gate)
    buf = jnp.concatenate([conv_ctx.astype(a.dtype), a], axis=1)
    c = jax.nn.silu(layer_norm(depthwise_causal(buf, conv_k, conv_b), conv_ln_g, conv_ln_b))
    u = z[..., :D_SGU]
    v = layer_norm(z[..., D_SGU:], sgu_ln_g, sgu_ln_b)
    s = u * sgu_mix(v, w_sgu, b_sgu)
    mix = jnp.concatenate([c, s], axis=-1) @ w_out
    x = layer_norm(ALPHA * x + mix, ln2_g, ln2_b)
    x = layer_norm(ALPHA * x + 0.5 * swiglu(x, w_ffn2_gate, w_ffn2_up, w_ffn2_down), ln3_g, ln3_b)
    return x, buf[:, -CONV_CTX:], v


def setup_inputs(seed: int = 0) -> dict:
    key = jax.random.key(seed)
    ks = jax.random.split(key, 32)
    f32 = jnp.float32
    n = lambda k, shape, s: (jax.random.normal(k, shape, f32) * s)
    L = DEPTH
    return {
        "x_prompt": n(ks[0], (BATCH, SEQ, D_MODEL), 1.0),
        "x_sample": n(ks[1], (DEC_BATCH, DEC_SEQ, D_MODEL), 1.0),
        "cache_conv": n(ks[2], (L, DEC_BATCH, CONV_CTX, D_CONV), 0.5),
        "w_ffn1_gate": n(ks[3], (L, D_MODEL, D_FF), D_MODEL ** -0.5),
        "w_ffn1_up": n(ks[4], (L, D_MODEL, D_FF), D_MODEL ** -0.5),
        "w_ffn1_down": n(ks[5], (L, D_FF, D_MODEL), BETA * D_FF ** -0.5),
        "ln1_g": 1.0 + n(ks[6], (L, D_MODEL), 0.01),
        "ln1_b": n(ks[7], (L, D_MODEL), 0.01),
        "w_in": n(ks[8], (L, D_MODEL, 2 * D_CONV + 2 * D_SGU), D_MODEL ** -0.5),
        "conv_k": n(ks[9], (L, CONV_WIDTH, D_CONV), CONV_WIDTH ** -0.5),
        "conv_b": n(ks[10], (L, D_CONV), 0.01),
        "conv_ln_g": 1.0 + n(ks[11], (L, D_CONV), 0.01),
        "conv_ln_b": n(ks[12], (L, D_CONV), 0.01),
        "sgu_ln_g": 1.0 + n(ks[13], (L, D_SGU), 0.01),
        "sgu_ln_b": n(ks[14], (L, D_SGU), 0.01),
        "w_sgu": n(ks[15], (L, N_SGU_HEADS, SGU_CHUNK, SGU_CHUNK), SGU_CHUNK ** -0.5),
        "b_sgu": 1.0 + n(ks[16], (L, N_SGU_HEADS, SGU_CHUNK), 0.01),
        "w_out": n(ks[17], (L, D_MIX, D_MODEL), BETA * D_MIX ** -0.5),
        "ln2_g": 1.0 + n(ks[18], (L, D_MODEL), 0.01),
        "ln2_b": n(ks[19], (L, D_MODEL), 0.01),
        "w_ffn2_gate": n(ks[20], (L, D_MODEL, D_FF), D_MODEL ** -0.5),
        "w_ffn2_up": n(ks[21], (L, D_MODEL, D_FF), D_MODEL ** -0.5),
        "w_ffn2_down": n(ks[22], (L, D_FF, D_MODEL), BETA * D_FF ** -0.5),
        "ln3_g": 1.0 + n(ks[23], (L, D_MODEL), 0.01),
        "ln3_b": n(ks[24], (L, D_MODEL), 0.01),
    }


def reference(x_prompt, x_sample, cache_conv, w_ffn1_gate, w_ffn1_up, w_ffn1_down, ln1_g, ln1_b, w_in,
              conv_k, conv_b, conv_ln_g, conv_ln_b, sgu_ln_g, sgu_ln_b, w_sgu, b_sgu, w_out, ln2_g, ln2_b,
              w_ffn2_gate, w_ffn2_up, w_ffn2_down, ln3_g, ln3_b):
    xp = x_prompt
    xs = x_sample
    conv_p, conv_s, sgu_s = [], [], []
    for l in range(DEPTH):
        params = (w_ffn1_gate[l], w_ffn1_up[l], w_ffn1_down[l], ln1_g[l], ln1_b[l], w_in[l], conv_k[l],
                  conv_b[l], conv_ln_g[l], conv_ln_b[l], sgu_ln_g[l], sgu_ln_b[l], w_sgu[l], b_sgu[l],
                  w_out[l], ln2_g[l], ln2_b[l], w_ffn2_gate[l], w_ffn2_up[l], w_ffn2_down[l],
                  ln3_g[l], ln3_b[l])
        zero_ctx = jnp.zeros((xp.shape[0], CONV_CTX, D_CONV), xp.dtype)
        xp, cp, _ = trunk_layer(xp, zero_ctx, *params)
        xs, cs, vs = trunk_layer(xs, cache_conv[l], *params)
        conv_p.append(cp)
        conv_s.append(cs)
        sgu_s.append(vs)
    state_conv_prompt = jnp.stack(conv_p, axis=0)
    state_conv_sample = jnp.stack(conv_s, axis=0)
    state_sgu_v_sample = jnp.stack(sgu_s, axis=0)
    return (xp, xs, state_conv_prompt, state_conv_sample, state_sgu_v_sample)
```

```python
import functools

import jax
import jax.numpy as jnp
from jax import lax
from jax.experimental import pallas as pl
from jax.experimental.pallas import tpu as pltpu

N_SGU_HEADS = 8
SGU_CHUNK = 128
LN_EPS = 1e-5

LANES = 128
SUBLANES = 8
VMEM_LIMIT_BYTES = 56 * 1024 * 1024

FFN_TOKEN_TILE = 512
FFN_HIDDEN_CHUNK = 256
MIXER_SEQ_TILE = 512
CONV_ROW_BLOCK = 32
CTX_ROWS = 32

f32 = jnp.float32
bf16 = jnp.bfloat16


def _layer_norm(y, g, b):
    mu = jnp.mean(y, axis=-1, keepdims=True)
    d = y - mu
    var = jnp.mean(d * d, axis=-1, keepdims=True)
    return d * lax.rsqrt(var + LN_EPS) * g + b


def _gelu(x):
    return 0.5 * x * (1.0 + lax.erf(x * (2.0 ** -0.5)))


def _resident(shape):
    zeros = (0,) * len(shape)
    return pl.BlockSpec(shape, lambda *_: zeros, pipeline_mode=pl.Buffered(1))


def _ffn_kernel(x_ref, wg_ref, wu_ref, wd_ref, g_ref, b_ref, o_ref, h_ref, *, alpha):
    x = x_ref[...]
    xb = x.astype(bf16)
    d_ff = wg_ref.shape[1]
    for c in range(0, d_ff, FFN_HIDDEN_CHUNK):
        gate = jnp.dot(xb, wg_ref[:, c:c + FFN_HIDDEN_CHUNK], preferred_element_type=f32)
        up = jnp.dot(xb, wu_ref[:, c:c + FFN_HIDDEN_CHUNK], preferred_element_type=f32)
        h_ref[:, c:c + FFN_HIDDEN_CHUNK] = (jax.nn.silu(gate) * up).astype(bf16)
    f = jnp.dot(h_ref[...], wd_ref[...], preferred_element_type=f32)
    o_ref[...] = _layer_norm(alpha * x + 0.5 * f, g_ref[...], b_ref[...])


def _ffn(x, wg, wu, wd, g, b, *, alpha):
    n_tok, d_model = x.shape
    d_ff = wg.shape[1]
    tile = FFN_TOKEN_TILE
    assert n_tok % tile == 0 and d_ff % FFN_HIDDEN_CHUNK == 0
    row = pl.BlockSpec((tile, d_model), lambda i: (i, 0))
    return pl.pallas_call(
        functools.partial(_ffn_kernel, alpha=alpha),
        grid=(n_tok // tile,),
        in_specs=[row, _resident((d_model, d_ff)), _resident((d_model, d_ff)), _resident((d_ff, d_model)),
                  _resident((1, d_model)), _resident((1, d_model))],
        out_specs=row,
        out_shape=jax.ShapeDtypeStruct((n_tok, d_model), f32),
        scratch_shapes=[pltpu.VMEM((tile, d_ff), bf16)],
        compiler_params=pltpu.CompilerParams(dimension_semantics=("arbitrary",),
                                             vmem_limit_bytes=VMEM_LIMIT_BYTES),
        name="ffn",
    )(x, wg, wu, wd, g, b)


def _mixer_kernel(x_ref, ctx_ref, w_in_ref, ck_ref, cb_ref, cg_ref, cbeta_ref, sg_ref, sbeta_ref, ws_ref, bs_ref,
                  w_out_ref, g_ref, b_ref, *rest, alpha, chunk, emit_v):
    if emit_v:
        y_ref, state_ref, v_ref, buf_ref, mix_ref = rest
    else:
        y_ref, state_ref, buf_ref, mix_ref = rest
    tile = x_ref.shape[0]
    d_conv = ck_ref.shape[1]
    d_sgu = sg_ref.shape[1]
    conv_width = ck_ref.shape[0]
    conv_ctx = conv_width - 1
    pad = CTX_ROWS - conv_ctx

    @pl.when(pl.program_id(1) == 0)
    def _():
        buf_ref[pl.ds(pad, conv_ctx), :] = ctx_ref[...]

    x = x_ref[...]
    xb = x.astype(bf16)

    a_val = jnp.dot(xb, w_in_ref[:, 0:d_conv], preferred_element_type=f32)
    a_gate = jnp.dot(xb, w_in_ref[:, d_conv:2 * d_conv], preferred_element_type=f32)
    buf_ref[pl.ds(CTX_ROWS, tile), :] = a_val * jax.nn.sigmoid(a_gate)
    state_ref[...] = buf_ref[pl.ds(CTX_ROWS + tile - conv_ctx, conv_ctx), :]
    for r in range(0, tile, CONV_ROW_BLOCK):
        rows = min(CONV_ROW_BLOCK, tile - r)
        acc = jnp.broadcast_to(cb_ref[...], (rows, d_conv))
        for k in range(conv_width):
            acc = acc + buf_ref[pl.ds(r + pad + k, rows), :] * ck_ref[pl.ds(k, 1), :]
        c = jax.nn.silu(_layer_norm(acc, cg_ref[...], cbeta_ref[...]))
        mix_ref[pl.ds(r, rows), 0:d_conv] = c.astype(bf16)
    buf_ref[pl.ds(0, CTX_ROWS), :] = buf_ref[pl.ds(tile, CTX_ROWS), :]

    z_u = jnp.dot(xb, w_in_ref[:, 2 * d_conv:2 * d_conv + d_sgu], preferred_element_type=f32)
    z_v = jnp.dot(xb, w_in_ref[:, 2 * d_conv + d_sgu:], preferred_element_type=f32)
    u = _gelu(z_u)
    v = _layer_norm(_gelu(z_v), sg_ref[...], sbeta_ref[...])
    if emit_v:
        v_ref[...] = v
    vb = v.astype(bf16)
    tri = lax.broadcasted_iota(jnp.int32, (chunk, chunk), 0) >= lax.broadcasted_iota(jnp.int32, (chunk, chunk), 1)
    w_heads = [jnp.where(tri, ws_ref[h, 0:chunk, 0:chunk], 0.0).astype(bf16) for h in range(N_SGU_HEADS)]
    head_dim = d_sgu // N_SGU_HEADS
    heads_per_group = LANES // head_dim
    lane_head = lax.broadcasted_iota(jnp.int32, (chunk, LANES), 1) // head_dim
    for n in range(0, tile, chunk):
        for p in range(d_sgu // LANES):
            cols = slice(p * LANES, (p + 1) * LANES)
            vp = vb[n:n + chunk, cols]
            mixed = jnp.dot(w_heads[p * heads_per_group], vp, preferred_element_type=f32)
            for j in range(1, heads_per_group):
                other = jnp.dot(w_heads[p * heads_per_group + j], vp, preferred_element_type=f32)
                mixed = jnp.where(lane_head == j, other, mixed)
            s = u[n:n + chunk, cols] * (mixed + bs_ref[:, cols])
            mix_ref[pl.ds(n, chunk), d_conv + p * LANES:d_conv + (p + 1) * LANES] = s.astype(bf16)

    m = jnp.dot(mix_ref[...], w_out_ref[...], preferred_element_type=f32)
    y_ref[...] = _layer_norm(alpha * x + m, g_ref[...], b_ref[...])


def _mixer(x, ctx, w_in, conv_k, conv_b, conv_g, conv_beta, sgu_g, sgu_beta, w_sgu, b_sgu, w_out, g, b, *,
           alpha, emit_v):
    batch, seq, d_model = x.shape
    conv_width, d_conv = conv_k.shape
    conv_ctx = conv_width - 1
    d_sgu = sgu_g.shape[1]
    d_mix = w_out.shape[0]
    tile = min(seq, MIXER_SEQ_TILE)
    chunk = min(seq, SGU_CHUNK)
    assert seq % tile == 0 and tile % chunk == 0 and tile >= CTX_ROWS >= conv_ctx
    assert d_mix == d_conv + d_sgu and LANES % (d_sgu // N_SGU_HEADS) == 0
    bias = jnp.repeat(b_sgu[:, :chunk].T, d_sgu // N_SGU_HEADS, axis=1)

    row = pl.BlockSpec((None, tile, d_model), lambda bi, ti: (bi, ti, 0))
    per_seq = lambda width: pl.BlockSpec((None, conv_ctx, width), lambda bi, ti: (bi, 0, 0))
    out_specs = [row, per_seq(d_conv)]
    out_shape = [jax.ShapeDtypeStruct((batch, seq, d_model), f32),
                 jax.ShapeDtypeStruct((batch, conv_ctx, d_conv), f32)]
    if emit_v:
        out_specs.append(pl.BlockSpec((None, tile, d_sgu), lambda bi, ti: (bi, ti, 0)))
        out_shape.append(jax.ShapeDtypeStruct((batch, seq, d_sgu), f32))
    return pl.pallas_call(
        functools.partial(_mixer_kernel, alpha=alpha, chunk=chunk, emit_v=emit_v),
        grid=(batch, seq // tile),
        in_specs=[row, per_seq(d_conv), _resident(w_in.shape), _resident(conv_k.shape),
                  _resident((1, d_conv)), _resident((1, d_conv)), _resident((1, d_conv)),
                  _resident((1, d_sgu)), _resident((1, d_sgu)), _resident(w_sgu.shape), _resident(bias.shape),
                  _resident(w_out.shape), _resident((1, d_model)), _resident((1, d_model))],
        out_specs=out_specs,
        out_shape=out_shape,
        scratch_shapes=[pltpu.VMEM((CTX_ROWS + tile, d_conv), f32), pltpu.VMEM((tile, d_mix), bf16)],
        compiler_params=pltpu.CompilerParams(dimension_semantics=("arbitrary", "arbitrary"),
                                             vmem_limit_bytes=VMEM_LIMIT_BYTES),
        name="mixer",
    )(x, ctx, w_in, conv_k, conv_b, conv_g, conv_beta, sgu_g, sgu_beta, w_sgu, bias, w_out, g, b)


def kernel(x_prompt, x_sample, cache_conv, w_ffn1_gate, w_ffn1_up, w_ffn1_down, ln1_g, ln1_b, w_in, conv_k, conv_b,
           conv_ln_g, conv_ln_b, sgu_ln_g, sgu_ln_b, w_sgu, b_sgu, w_out, ln2_g, ln2_b, w_ffn2_gate, w_ffn2_up,
           w_ffn2_down, ln3_g, ln3_b):
    depth = w_in.shape[0]
    alpha = float((2 * depth) ** 0.25)
    batch, seq, d_model = x_prompt.shape
    dec_batch, dec_seq, _ = x_sample.shape
    conv_ctx, d_conv = cache_conv.shape[2:]
    row = lambda p: p[:, None, :]

    def ffn(x, l, wg, wu, wd, g, b):
        shape = x.shape
        y = _ffn(x.reshape(-1, d_model), wg[l], wu[l], wd[l], g[l], b[l], alpha=alpha)
        return y.reshape(shape)

    def mixer(x, ctx, l, emit_v):
        return _mixer(x, ctx, w_in_b[l], conv_k[l], conv_b_r[l], conv_g_r[l], conv_beta_r[l], sgu_g_r[l],
                      sgu_beta_r[l], w_sgu[l], b_sgu[l], w_out_b[l], ln2_g_r[l], ln2_b_r[l],
                      alpha=alpha, emit_v=emit_v)

    wg1, wu1, wd1 = (w.astype(bf16) for w in (w_ffn1_gate, w_ffn1_up, w_ffn1_down))
    wg2, wu2, wd2 = (w.astype(bf16) for w in (w_ffn2_gate, w_ffn2_up, w_ffn2_down))
    w_in_b, w_out_b = w_in.astype(bf16), w_out.astype(bf16)
    ln1_g_r, ln1_b_r, ln2_g_r, ln2_b_r, ln3_g_r, ln3_b_r = map(row, (ln1_g, ln1_b, ln2_g, ln2_b, ln3_g, ln3_b))
    conv_b_r, conv_g_r, conv_beta_r, sgu_g_r, sgu_beta_r = map(row, (conv_b, conv_ln_g, conv_ln_b, sgu_ln_g, sgu_ln_b))

    zero_ctx = jnp.zeros((batch, conv_ctx, d_conv), f32)
    xp, xs = x_prompt, x_sample
    conv_p, conv_s, sgu_s = [], [], []
    for l in range(depth):
        xp = ffn(xp, l, wg1, wu1, wd1, ln1_g_r, ln1_b_r)
        xs = ffn(xs, l, wg1, wu1, wd1, ln1_g_r, ln1_b_r)
        xp, cp = mixer(xp, zero_ctx, l, emit_v=False)
        xs, cs, vs = mixer(xs, cache_conv[l], l, emit_v=True)
        xp = ffn(xp, l, wg2, wu2, wd2, ln3_g_r, ln3_b_r)
        xs = ffn(xs, l, wg2, wu2, wd2, ln3_g_r, ln3_b_r)
        conv_p.append(cp)
        conv_s.append(cs)
        sgu_s.append(vs)
    return (xp, xs, jnp.stack(conv_p, axis=0), jnp.stack(conv_s, axis=0), jnp.stack(sgu_s, axis=0))
```

```python
import functools

import jax
import jax.numpy as jnp
from jax import lax
from jax.experimental import pallas as pl
from jax.experimental.pallas import tpu as pltpu

N_SGU_HEADS = 8
SGU_CHUNK = 128
LN_EPS = 1e-5

LANES = 128
SUBLANES = 8
VMEM_LIMIT_BYTES = 56 * 1024 * 1024

SUB_TILE = 256
FFN_TOKEN_TILE = 1024
MIXER_TOKEN_TILE = 512
FFN_HIDDEN_CHUNK = 256
CONV_ROW_BLOCK = 64
CTX_ROWS = 32

f32 = jnp.float32
bf16 = jnp.bfloat16


def _layer_norm(y, g, b):
    mu = jnp.mean(y, axis=-1, keepdims=True)
    d = y - mu
    var = jnp.mean(d * d, axis=-1, keepdims=True)
    return d * lax.rsqrt(var + LN_EPS) * g + b


def _gelu(x):
    return 0.5 * x * (1.0 + lax.erf(x * (2.0 ** -0.5)))


def _layer_block(layer, shape):
    zeros = (0,) * len(shape)
    return pl.BlockSpec((None,) + tuple(shape), lambda *_: (layer,) + zeros, pipeline_mode=pl.Buffered(1))


def _ffn_tile(x_ref, wg_ref, wu_ref, wd_ref, g_ref, b_ref, o_ref, h_ref, alpha):
    d_ff = wg_ref.shape[1]
    for i, r in enumerate(range(0, x_ref.shape[0], SUB_TILE)):
        x = x_ref[pl.ds(r, SUB_TILE), :]
        xb = x.astype(bf16)
        h = h_ref.at[i % h_ref.shape[0]]
        for c in range(0, d_ff, FFN_HIDDEN_CHUNK):
            gate = jnp.dot(xb, wg_ref[:, c:c + FFN_HIDDEN_CHUNK], preferred_element_type=f32)
            up = jnp.dot(xb, wu_ref[:, c:c + FFN_HIDDEN_CHUNK], preferred_element_type=f32)
            h[:, c:c + FFN_HIDDEN_CHUNK] = (jax.nn.silu(gate) * up).astype(bf16)
        f = jnp.dot(h[...], wd_ref[...], preferred_element_type=f32)
        o_ref[pl.ds(r, SUB_TILE), :] = _layer_norm(alpha * x + 0.5 * f, g_ref[...], b_ref[...])


def _ffn_kernel(xp_ref, xs_ref, wg_ref, wu_ref, wd_ref, g_ref, b_ref, op_ref, os_ref, h_ref, *, alpha, prompt_tiles):
    i = pl.program_id(0)

    @pl.when(i < prompt_tiles)
    def _():
        _ffn_tile(xp_ref, wg_ref, wu_ref, wd_ref, g_ref, b_ref, op_ref, h_ref, alpha)

    @pl.when(i >= prompt_tiles)
    def _():
        _ffn_tile(xs_ref, wg_ref, wu_ref, wd_ref, g_ref, b_ref, os_ref, h_ref, alpha)


def _ffn(xp, xs, layer, wg, wu, wd, g, b, *, alpha):
    d_model = xp.shape[1]
    d_ff = wg.shape[2]
    tile = FFN_TOKEN_TILE
    assert xp.shape[0] % tile == 0 and xs.shape[0] % tile == 0 and d_ff % FFN_HIDDEN_CHUNK == 0
    assert tile % SUB_TILE == 0
    prompt_tiles, sample_tiles = xp.shape[0] // tile, xs.shape[0] // tile
    prompt_rows = pl.BlockSpec((tile, d_model), lambda i: (jnp.minimum(i, prompt_tiles - 1), 0))
    sample_rows = pl.BlockSpec((tile, d_model), lambda i: (jnp.maximum(i - prompt_tiles, 0), 0),
                               pipeline_mode=pl.Buffered(1) if sample_tiles == 1 else None)
    return pl.pallas_call(
        functools.partial(_ffn_kernel, alpha=alpha, prompt_tiles=prompt_tiles),
        grid=(prompt_tiles + sample_tiles,),
        in_specs=[prompt_rows, sample_rows, _layer_block(layer, (d_model, d_ff)), _layer_block(layer, (d_model, d_ff)),
                  _layer_block(layer, (d_ff, d_model)), _layer_block(layer, (1, d_model)),
                  _layer_block(layer, (1, d_model))],
        out_specs=[prompt_rows, sample_rows],
        out_shape=[jax.ShapeDtypeStruct(xp.shape, f32), jax.ShapeDtypeStruct(xs.shape, f32)],
        scratch_shapes=[pltpu.VMEM((2, SUB_TILE, d_ff), bf16)],
        compiler_params=pltpu.CompilerParams(dimension_semantics=("arbitrary",),
                                             vmem_limit_bytes=VMEM_LIMIT_BYTES),
        name="ffn",
    )(xp, xs, wg, wu, wd, g, b)


def _mixer_kernel(x_ref, ctx_ref, w_in_ref, ck_ref, cb_ref, cg_ref, cbeta_ref, sg_ref, sbeta_ref, ws_ref, bs_ref,
                  w_out_ref, g_ref, b_ref, *rest, alpha, mix_len, carry, emit_v):
    if emit_v:
        y_ref, state_ref, v_ref, buf_ref, shift_ref, u_ref, vb_ref, mix_ref = rest
    else:
        y_ref, state_ref, buf_ref, shift_ref, u_ref, vb_ref, mix_ref = rest
    n_seg, seg, d_model = x_ref.shape
    tile = n_seg * seg
    conv_width, d_conv = ck_ref.shape
    d_sgu = sg_ref.shape[1]
    conv_ctx = conv_width - 1
    pad = CTX_ROWS - conv_ctx
    stride = CTX_ROWS + seg
    sub = min(SUB_TILE, tile)
    piece_rows = min(sub, seg)
    pieces = [[(t, (t // seg) * stride + CTX_ROWS + t % seg, piece_rows) for t in range(t0, t0 + sub, piece_rows)]
              for t0 in range(0, tile, sub)]

    def load_rows(ref, t, rows):
        return ref[t // seg, pl.ds(t % seg, rows), :]

    head_dim = d_sgu // N_SGU_HEADS
    heads_per_group = LANES // head_dim
    lane_head = (lax.broadcasted_iota(jnp.int32, (SGU_CHUNK, d_sgu), 1) // head_dim) % heads_per_group

    @pl.when(pl.program_id(1) == 0)
    def _():
        for s in range(n_seg):
            buf_ref[pl.ds(s * stride, SUBLANES), :] = jnp.zeros((SUBLANES, d_conv), f32)
            buf_ref[pl.ds(s * stride + pad, conv_ctx), :] = ctx_ref[s]

    for sub_pieces in pieces:
        t0 = sub_pieces[0][0]
        xb = jnp.concatenate([load_rows(x_ref, t, n) for t, _, n in sub_pieces], axis=0).astype(bf16)
        a_val = jnp.dot(xb, w_in_ref[:, 0:d_conv], preferred_element_type=f32)
        a_gate = jnp.dot(xb, w_in_ref[:, d_conv:2 * d_conv], preferred_element_type=f32)
        a = a_val * jax.nn.sigmoid(a_gate)
        for i, (t, row, n) in enumerate(sub_pieces):
            buf_ref[pl.ds(row, n), :] = a[i * n:(i + 1) * n]
        z_u = jnp.dot(xb, w_in_ref[:, 2 * d_conv:2 * d_conv + d_sgu], preferred_element_type=f32)
        u_ref[pl.ds(t0, sub), :] = _gelu(z_u)
        z_v = jnp.dot(xb, w_in_ref[:, 2 * d_conv + d_sgu:], preferred_element_type=f32)
        v = _layer_norm(_gelu(z_v), sg_ref[...], sbeta_ref[...])
        for c0 in range(0, sub, SGU_CHUNK):
            for j in range(heads_per_group):
                masked = jnp.where(lane_head == j, v[c0:c0 + SGU_CHUNK], 0.0)
                vb_ref[pl.ds(heads_per_group * (t0 + c0) + j * SGU_CHUNK, SGU_CHUNK), :] = masked.astype(bf16)
        if emit_v:
            for i, (t, row, n) in enumerate(sub_pieces):
                v_ref[t // seg, pl.ds(t % seg, n), :] = v[i * n:(i + 1) * n]
    for s in range(n_seg):
        state_ref[s] = buf_ref[pl.ds((s + 1) * stride - conv_ctx, conv_ctx), :]

    t_idx = lax.broadcasted_iota(jnp.int32, (SGU_CHUNK, SGU_CHUNK), 0)
    s_idx = lax.broadcasted_iota(jnp.int32, (SGU_CHUNK, SGU_CHUNK), 1)
    allowed = (t_idx // mix_len == s_idx // mix_len) & (t_idx >= s_idx)
    w_heads = [jnp.where(allowed, ws_ref[h], 0.0).astype(bf16) for h in range(N_SGU_HEADS)]
    w_groups = [jnp.concatenate(w_heads[p:p + heads_per_group], axis=1)
                for p in range(0, N_SGU_HEADS, heads_per_group)]

    shifted_to = 0
    for sub_pieces in pieces:
        t0 = sub_pieces[0][0]
        shift_end = sub_pieces[-1][1] + sub_pieces[-1][2] - SUBLANES
        for g in range(0, d_conv, LANES):
            cols = slice(g, g + LANES)
            for r0 in range(shifted_to, shift_end, CONV_ROW_BLOCK):
                rows = min(CONV_ROW_BLOCK, shift_end - r0)
                blk = buf_ref[pl.ds(r0, rows + SUBLANES), cols]
                for r in range(1, SUBLANES):
                    shift_ref[r - 1, pl.ds(r0, rows), cols] = pltpu.roll(blk, rows + SUBLANES - r, axis=0)[:rows]
        shifted_to = shift_end
        for t, row, n in sub_pieces:
            for b0 in range(0, n, CONV_ROW_BLOCK):
                rows = min(CONV_ROW_BLOCK, n - b0)
                parts = []
                for g in range(0, d_conv, LANES):
                    cols = slice(g, g + LANES)
                    acc = jnp.broadcast_to(cb_ref[:, cols], (rows, LANES))
                    for k in range(conv_width):
                        q, r = divmod(pad + k, SUBLANES)
                        row0 = row - CTX_ROWS + b0 + q * SUBLANES
                        src = (buf_ref[pl.ds(row0, rows), cols] if r == 0
                               else shift_ref[r - 1, pl.ds(row0, rows), cols])
                        acc = acc + src * ck_ref[pl.ds(k, 1), cols]
                    parts.append(acc)
                c = jax.nn.silu(_layer_norm(jnp.concatenate(parts, axis=1), cg_ref[...], cbeta_ref[...]))
                mix_ref[pl.ds(t + b0, rows), 0:d_conv] = c.astype(bf16)

        for n0 in range(t0, t0 + sub, SGU_CHUNK):
            for p in range(d_sgu // LANES):
                cols = slice(p * LANES, (p + 1) * LANES)
                stacked = vb_ref[pl.ds(heads_per_group * n0, heads_per_group * SGU_CHUNK), cols]
                mixed = jnp.dot(w_groups[p], stacked, preferred_element_type=f32)
                s_out = u_ref[pl.ds(n0, SGU_CHUNK), cols] * (mixed + bs_ref[:, cols])
                mix_ref[pl.ds(n0, SGU_CHUNK), d_conv + p * LANES:d_conv + (p + 1) * LANES] = s_out.astype(bf16)

        m = jnp.dot(mix_ref[pl.ds(t0, sub), :], w_out_ref[...], preferred_element_type=f32)
        x = jnp.concatenate([load_rows(x_ref, t, n) for t, _, n in sub_pieces], axis=0)
        y = _layer_norm(alpha * x + m, g_ref[...], b_ref[...])
        for i, (t, row, n) in enumerate(sub_pieces):
            y_ref[t // seg, pl.ds(t % seg, n), :] = y[i * n:(i + 1) * n]
    if carry:
        buf_ref[pl.ds(0, CTX_ROWS), :] = buf_ref[pl.ds(seg, CTX_ROWS), :]


def _mixer(x, ctx, layer, w_in, conv_k, conv_b, conv_g, conv_beta, sgu_g, sgu_beta, w_sgu, b_sgu, w_out, g, b, *,
           alpha, emit_v):
    batch, seq, d_model = x.shape
    conv_width, d_conv = conv_k.shape[1:]
    conv_ctx = conv_width - 1
    d_sgu = sgu_g.shape[2]
    d_mix = w_out.shape[1]
    head_dim = d_sgu // N_SGU_HEADS
    tile = MIXER_TOKEN_TILE
    assert d_mix == d_conv + d_sgu and LANES % head_dim == 0 and CTX_ROWS >= conv_ctx
    if seq >= tile:
        assert seq % tile == 0
        n_seg, seg = 1, tile
    else:
        assert tile % seq == 0 and batch % (tile // seq) == 0
        n_seg, seg = tile // seq, seq
    assert seg >= CTX_ROWS and seg % CONV_ROW_BLOCK == 0 and tile % SUB_TILE == 0
    assert seg % SUB_TILE == 0 or SUB_TILE % seg == 0
    mix_len = min(seq, SGU_CHUNK)
    assert SGU_CHUNK % mix_len == 0 and SUB_TILE % SGU_CHUNK == 0
    reps = SGU_CHUNK // mix_len
    w_tiled = jnp.tile(w_sgu[layer, :, :mix_len, :mix_len], (1, reps, reps))
    bias = jnp.tile(jnp.repeat(b_sgu[layer, :, :mix_len].T, head_dim, axis=1), (reps, 1))

    grid = (batch // n_seg, seq // seg)
    rows = lambda width: pl.BlockSpec((n_seg, seg, width), lambda bi, ti: (bi, ti, 0))
    per_seq = pl.BlockSpec((n_seg, conv_ctx, d_conv), lambda bi, ti: (bi, 0, 0))
    whole = lambda arr: pl.BlockSpec(arr.shape, lambda *_: (0,) * arr.ndim, pipeline_mode=pl.Buffered(1))
    out_specs = [rows(d_model), per_seq]
    out_shape = [jax.ShapeDtypeStruct(x.shape, f32), jax.ShapeDtypeStruct((batch, conv_ctx, d_conv), f32)]
    if emit_v:
        out_specs.append(rows(d_sgu))
        out_shape.append(jax.ShapeDtypeStruct((batch, seq, d_sgu), f32))
    buf_rows = n_seg * (CTX_ROWS + seg)
    return pl.pallas_call(
        functools.partial(_mixer_kernel, alpha=alpha, mix_len=mix_len, carry=grid[1] > 1, emit_v=emit_v),
        grid=grid,
        in_specs=[rows(d_model), per_seq, _layer_block(layer, w_in.shape[1:]), _layer_block(layer, conv_k.shape[1:]),
                  _layer_block(layer, (1, d_conv)), _layer_block(layer, (1, d_conv)), _layer_block(layer, (1, d_conv)),
                  _layer_block(layer, (1, d_sgu)), _layer_block(layer, (1, d_sgu)), whole(w_tiled), whole(bias),
                  _layer_block(layer, w_out.shape[1:]), _layer_block(layer, (1, d_model)),
                  _layer_block(layer, (1, d_model))],
        out_specs=out_specs,
        out_shape=out_shape,
        scratch_shapes=[pltpu.VMEM((buf_rows, d_conv), f32),
                        pltpu.VMEM((SUBLANES - 1, buf_rows, d_conv), f32),
                        pltpu.VMEM((tile, d_sgu), f32),
                        pltpu.VMEM((LANES // head_dim * tile, d_sgu), bf16),
                        pltpu.VMEM((tile, d_mix), bf16)],
        compiler_params=pltpu.CompilerParams(dimension_semantics=("arbitrary", "arbitrary"),
                                             vmem_limit_bytes=VMEM_LIMIT_BYTES),
        name="mixer",
    )(x, ctx, w_in, conv_k, conv_b, conv_g, conv_beta, sgu_g, sgu_beta, w_tiled, bias, w_out, g, b)


def kernel(x_prompt, x_sample, cache_conv, w_ffn1_gate, w_ffn1_up, w_ffn1_down, ln1_g, ln1_b, w_in, conv_k, conv_b,
           conv_ln_g, conv_ln_b, sgu_ln_g, sgu_ln_b, w_sgu, b_sgu, w_out, ln2_g, ln2_b, w_ffn2_gate, w_ffn2_up,
           w_ffn2_down, ln3_g, ln3_b):
    depth = w_in.shape[0]
    alpha = float((2 * depth) ** 0.25)
    batch, seq, d_model = x_prompt.shape
    conv_ctx, d_conv = cache_conv.shape[2:]
    row = lambda p: p[:, None, :]

    wg1, wu1, wd1 = (w.astype(bf16) for w in (w_ffn1_gate, w_ffn1_up, w_ffn1_down))
    wg2, wu2, wd2 = (w.astype(bf16) for w in (w_ffn2_gate, w_ffn2_up, w_ffn2_down))
    w_in_b, w_out_b = w_in.astype(bf16), w_out.astype(bf16)
    ln1, ln2, ln3 = (row(ln1_g), row(ln1_b)), (row(ln2_g), row(ln2_b)), (row(ln3_g), row(ln3_b))
    conv_vecs = (row(conv_b), row(conv_ln_g), row(conv_ln_b))
    sgu_vecs = (row(sgu_ln_g), row(sgu_ln_b))

    def ffn(xp, xs, l, wg, wu, wd, ln):
        yp, ys = _ffn(xp.reshape(-1, d_model), xs.reshape(-1, d_model), l, wg, wu, wd, *ln, alpha=alpha)
        return yp.reshape(xp.shape), ys.reshape(xs.shape)

    def mixer(x, ctx, l, emit_v):
        return _mixer(x, ctx, l, w_in_b, conv_k, *conv_vecs, *sgu_vecs, w_sgu, b_sgu, w_out_b, *ln2,
                      alpha=alpha, emit_v=emit_v)

    zero_ctx = jnp.zeros((batch, conv_ctx, d_conv), f32)
    xp, xs = x_prompt, x_sample
    conv_p, conv_s, sgu_s = [], [], []
    for l in range(depth):
        xp, xs = ffn(xp, xs, l, wg1, wu1, wd1, ln1)
        xp, cp = mixer(xp, zero_ctx, l, emit_v=False)
        xs, cs, vs = mixer(xs, cache_conv[l], l, emit_v=True)
        xp, xs = ffn(xp, xs, l, wg2, wu2, wd2, ln3)
        conv_p.append(cp)
        conv_s.append(cs)
        sgu_s.append(vs)
    return (xp, xs, jnp.stack(conv_p, axis=0), jnp.stack(conv_s, axis=0), jnp.stack(sgu_s, axis=0))
```

```python
import functools

import jax
import jax.numpy as jnp
from jax import lax
from jax.experimental import pallas as pl
from jax.experimental.pallas import tpu as pltpu

N_SGU_HEADS = 8
SGU_CHUNK = 128
LN_EPS = 1e-5

LANES = 128
SUBLANES = 8
BF16_SUBLANES = 16
VMEM_LIMIT_BYTES = 58 * 1024 * 1024

SUB_TILE = 512
FFN_TOKEN_TILE = 1024
MIXER_TOKEN_TILE = 512
FFN_HIDDEN_CHUNK = 256
FFN_WEIGHT_PREP_STEPS = 16
CONV_ROW_BLOCK = 64
CTX_ROWS = 32

f32 = jnp.float32
bf16 = jnp.bfloat16


def _layer_norm(y, g, b):
    mu = jnp.mean(y, axis=-1, keepdims=True)
    d = y - mu
    var = jnp.mean(d * d, axis=-1, keepdims=True)
    return d * lax.rsqrt(var + LN_EPS) * g + b


def _gelu(x):
    return 0.5 * x * (1.0 + lax.erf(x * (2.0 ** -0.5)))


def _layer_block(layer, shape):
    zeros = (0,) * len(shape)
    return pl.BlockSpec((None,) + tuple(shape), lambda *_: (layer,) + zeros, pipeline_mode=pl.Buffered(1))


def _ffn_tile(x_ref, wg_ref, wu_ref, wd_ref, g_ref, b_ref, o_ref, h_ref, alpha):
    d_ff = wg_ref.shape[1]
    for r in range(0, x_ref.shape[0], SUB_TILE):
        x = x_ref[pl.ds(r, SUB_TILE), :]
        xb = x.astype(bf16)
        for c in range(0, d_ff, FFN_HIDDEN_CHUNK):
            gate = jnp.dot(xb, wg_ref[:, c:c + FFN_HIDDEN_CHUNK], preferred_element_type=f32)
            up = jnp.dot(xb, wu_ref[:, c:c + FFN_HIDDEN_CHUNK], preferred_element_type=f32)
            h_ref[:, c:c + FFN_HIDDEN_CHUNK] = (jax.nn.silu(gate) * up).astype(bf16)
        f = jnp.dot(h_ref[...], wd_ref[...], preferred_element_type=f32)
        o_ref[pl.ds(r, SUB_TILE), :] = _layer_norm(alpha * x + 0.5 * f, g_ref[...], b_ref[...])


def _ffn_kernel(xp_ref, xs_ref, wg32_ref, wu32_ref, wd32_ref, g_ref, b_ref, op_ref, os_ref, wg_ref, wu_ref, wd_ref,
                h_ref, *, alpha, prep_steps, prompt_tiles):
    i = pl.program_id(0)

    @pl.when(i < prep_steps)
    def _():
        for src, dst in ((wg32_ref, wg_ref), (wu32_ref, wu_ref), (wd32_ref, wd_ref)):
            rows = src.shape[0]
            dst[pl.ds(pl.multiple_of(i * rows, rows), rows), :] = src[...].astype(bf16)

    @pl.when((i >= prep_steps) & (i < prep_steps + prompt_tiles))
    def _():
        _ffn_tile(xp_ref, wg_ref, wu_ref, wd_ref, g_ref, b_ref, op_ref, h_ref, alpha)

    @pl.when(i >= prep_steps + prompt_tiles)
    def _():
        _ffn_tile(xs_ref, wg_ref, wu_ref, wd_ref, g_ref, b_ref, os_ref, h_ref, alpha)


def _ffn(xp, xs, layer, wg, wu, wd, g, b, *, alpha):
    d_model = xp.shape[1]
    d_ff = wg.shape[2]
    tile = FFN_TOKEN_TILE
    prep = FFN_WEIGHT_PREP_STEPS
    assert xp.shape[0] % tile == 0 and xs.shape[0] % tile == 0 and d_ff % FFN_HIDDEN_CHUNK == 0
    assert tile % SUB_TILE == 0 and d_model % (prep * BF16_SUBLANES) == 0 and d_ff % (prep * BF16_SUBLANES) == 0
    prompt_tiles, sample_tiles = xp.shape[0] // tile, xs.shape[0] // tile
    prompt_rows = pl.BlockSpec((tile, d_model), lambda i: (jnp.clip(i - prep, 0, prompt_tiles - 1), 0))
    sample_index = lambda i: (jnp.clip(i - prep - prompt_tiles, 0, sample_tiles - 1), 0)
    sample_in = pl.BlockSpec((tile, d_model), sample_index, pipeline_mode=pl.Buffered(1) if sample_tiles == 1 else None)
    sample_out = pl.BlockSpec((tile, d_model), sample_index)
    slab = lambda rows, cols: pl.BlockSpec((None, rows // prep, cols), lambda i: (layer, jnp.minimum(i, prep - 1), 0))
    return pl.pallas_call(
        functools.partial(_ffn_kernel, alpha=alpha, prep_steps=prep, prompt_tiles=prompt_tiles),
        grid=(prep + prompt_tiles + sample_tiles,),
        in_specs=[prompt_rows, sample_in, slab(d_model, d_ff), slab(d_model, d_ff), slab(d_ff, d_model),
                  _layer_block(layer, (1, d_model)), _layer_block(layer, (1, d_model))],
        out_specs=[prompt_rows, sample_out],
        out_shape=[jax.ShapeDtypeStruct(xp.shape, f32), jax.ShapeDtypeStruct(xs.shape, f32)],
        scratch_shapes=[pltpu.VMEM((d_model, d_ff), bf16), pltpu.VMEM((d_model, d_ff), bf16),
                        pltpu.VMEM((d_ff, d_model), bf16), pltpu.VMEM((SUB_TILE, d_ff), bf16)],
        compiler_params=pltpu.CompilerParams(dimension_semantics=("arbitrary",),
                                             vmem_limit_bytes=VMEM_LIMIT_BYTES),
        name="ffn",
    )(xp, xs, wg, wu, wd, g, b)


def _mixer_kernel(x_ref, ctx_ref, w_in_ref, ck_ref, cb_ref, cg_ref, cbeta_ref, sg_ref, sbeta_ref, ws_ref, bs_ref,
                  w_out_ref, g_ref, b_ref, *rest, alpha, mix_len, carry, emit_v):
    if emit_v:
        y_ref, state_ref, v_ref, buf_ref, shift_ref, u_ref, vb_ref, mix_ref = rest
    else:
        y_ref, state_ref, buf_ref, shift_ref, u_ref, vb_ref, mix_ref = rest
    n_seg, seg, d_model = x_ref.shape
    tile = n_seg * seg
    conv_width, d_conv = ck_ref.shape
    d_sgu = sg_ref.shape[1]
    conv_ctx = conv_width - 1
    pad = CTX_ROWS - conv_ctx
    stride = CTX_ROWS + seg
    sub = min(SUB_TILE, tile)
    piece_rows = min(sub, seg)
    pieces = [[(t, (t // seg) * stride + CTX_ROWS + t % seg, piece_rows) for t in range(t0, t0 + sub, piece_rows)]
              for t0 in range(0, tile, sub)]

    def load_rows(ref, t, rows):
        return ref[t // seg, pl.ds(t % seg, rows), :]

    head_dim = d_sgu // N_SGU_HEADS
    heads_per_group = LANES // head_dim
    lane_head = (lax.broadcasted_iota(jnp.int32, (SGU_CHUNK, d_sgu), 1) // head_dim) % heads_per_group

    @pl.when(pl.program_id(1) == 0)
    def _():
        for s in range(n_seg):
            buf_ref[pl.ds(s * stride, SUBLANES), :] = jnp.zeros((SUBLANES, d_conv), f32)
            buf_ref[pl.ds(s * stride + pad, conv_ctx), :] = ctx_ref[s]

    for sub_pieces in pieces:
        t0 = sub_pieces[0][0]
        xb = jnp.concatenate([load_rows(x_ref, t, n) for t, _, n in sub_pieces], axis=0).astype(bf16)
        a_val = jnp.dot(xb, w_in_ref[:, 0:d_conv], preferred_element_type=f32)
        a_gate = jnp.dot(xb, w_in_ref[:, d_conv:2 * d_conv], preferred_element_type=f32)
        a = a_val * jax.nn.sigmoid(a_gate)
        for i, (t, row, n) in enumerate(sub_pieces):
            buf_ref[pl.ds(row, n), :] = a[i * n:(i + 1) * n]
        z_u = jnp.dot(xb, w_in_ref[:, 2 * d_conv:2 * d_conv + d_sgu], preferred_element_type=f32)
        u_ref[pl.ds(t0, sub), :] = _gelu(z_u)
        z_v = jnp.dot(xb, w_in_ref[:, 2 * d_conv + d_sgu:], preferred_element_type=f32)
        v = _layer_norm(_gelu(z_v), sg_ref[...], sbeta_ref[...])
        for c0 in range(0, sub, SGU_CHUNK):
            for j in range(heads_per_group):
                masked = jnp.where(lane_head == j, v[c0:c0 + SGU_CHUNK], 0.0)
                vb_ref[pl.ds(heads_per_group * (t0 + c0) + j * SGU_CHUNK, SGU_CHUNK), :] = masked.astype(bf16)
        if emit_v:
            for i, (t, row, n) in enumerate(sub_pieces):
                v_ref[t // seg, pl.ds(t % seg, n), :] = v[i * n:(i + 1) * n]
    for s in range(n_seg):
        state_ref[s] = buf_ref[pl.ds((s + 1) * stride - conv_ctx, conv_ctx), :]

    t_idx = lax.broadcasted_iota(jnp.int32, (SGU_CHUNK, SGU_CHUNK), 0)
    s_idx = lax.broadcasted_iota(jnp.int32, (SGU_CHUNK, SGU_CHUNK), 1)
    allowed = (t_idx // mix_len == s_idx // mix_len) & (t_idx >= s_idx)
    w_heads = [jnp.where(allowed, ws_ref[h], 0.0).astype(bf16) for h in range(N_SGU_HEADS)]
    w_groups = [jnp.concatenate(w_heads[p:p + heads_per_group], axis=1)
                for p in range(0, N_SGU_HEADS, heads_per_group)]

    shifted_to = 0
    for sub_pieces in pieces:
        t0 = sub_pieces[0][0]
        shift_end = sub_pieces[-1][1] + sub_pieces[-1][2] - SUBLANES
        for g in range(0, d_conv, LANES):
            cols = slice(g, g + LANES)
            for r0 in range(shifted_to, shift_end, CONV_ROW_BLOCK):
                rows = min(CONV_ROW_BLOCK, shift_end - r0)
                blk = buf_ref[pl.ds(r0, rows + SUBLANES), cols]
                for r in range(1, SUBLANES):
                    shift_ref[r - 1, pl.ds(r0, rows), cols] = pltpu.roll(blk, rows + SUBLANES - r, axis=0)[:rows]
        shifted_to = shift_end
        for t, row, n in sub_pieces:
            for b0 in range(0, n, CONV_ROW_BLOCK):
                rows = min(CONV_ROW_BLOCK, n - b0)
                parts = []
                for g in range(0, d_conv, LANES):
                    cols = slice(g, g + LANES)
                    acc = jnp.broadcast_to(cb_ref[:, cols], (rows, LANES))
                    for k in range(conv_width):
                        q, r = divmod(pad + k, SUBLANES)
                        row0 = row - CTX_ROWS + b0 + q * SUBLANES
                        src = (buf_ref[pl.ds(row0, rows), cols] if r == 0
                               else shift_ref[r - 1, pl.ds(row0, rows), cols])
                        acc = acc + src * ck_ref[pl.ds(k, 1), cols]
                    parts.append(acc)
                c = jax.nn.silu(_layer_norm(jnp.concatenate(parts, axis=1), cg_ref[...], cbeta_ref[...]))
                mix_ref[pl.ds(t + b0, rows), 0:d_conv] = c.astype(bf16)

        for n0 in range(t0, t0 + sub, SGU_CHUNK):
            for p in range(d_sgu // LANES):
                cols = slice(p * LANES, (p + 1) * LANES)
                stacked = vb_ref[pl.ds(heads_per_group * n0, heads_per_group * SGU_CHUNK), cols]
                mixed = jnp.dot(w_groups[p], stacked, preferred_element_type=f32)
                s_out = u_ref[pl.ds(n0, SGU_CHUNK), cols] * (mixed + bs_ref[:, cols])
                mix_ref[pl.ds(n0, SGU_CHUNK), d_conv + p * LANES:d_conv + (p + 1) * LANES] = s_out.astype(bf16)

        m = jnp.dot(mix_ref[pl.ds(t0, sub), :], w_out_ref[...], preferred_element_type=f32)
        x = jnp.concatenate([load_rows(x_ref, t, n) for t, _, n in sub_pieces], axis=0)
        y = _layer_norm(alpha * x + m, g_ref[...], b_ref[...])
        for i, (t, row, n) in enumerate(sub_pieces):
            y_ref[t // seg, pl.ds(t % seg, n), :] = y[i * n:(i + 1) * n]
    if carry:
        buf_ref[pl.ds(0, CTX_ROWS), :] = buf_ref[pl.ds(seg, CTX_ROWS), :]


def _mixer(x, ctx, layer, w_in, conv_k, conv_b, conv_g, conv_beta, sgu_g, sgu_beta, w_sgu, b_sgu, w_out, g, b, *,
           alpha, emit_v):
    batch, seq, d_model = x.shape
    conv_width, d_conv = conv_k.shape[1:]
    conv_ctx = conv_width - 1
    d_sgu = sgu_g.shape[2]
    d_mix = w_out.shape[1]
    head_dim = d_sgu // N_SGU_HEADS
    tile = MIXER_TOKEN_TILE
    assert d_mix == d_conv + d_sgu and LANES % head_dim == 0 and CTX_ROWS >= conv_ctx
    if seq >= tile:
        assert seq % tile == 0
        n_seg, seg = 1, tile
    else:
        assert tile % seq == 0 and batch % (tile // seq) == 0
        n_seg, seg = tile // seq, seq
    assert seg >= CTX_ROWS and seg % CONV_ROW_BLOCK == 0 and tile % SUB_TILE == 0
    assert seg % SUB_TILE == 0 or SUB_TILE % seg == 0
    mix_len = min(seq, SGU_CHUNK)
    assert SGU_CHUNK % mix_len == 0 and SUB_TILE % SGU_CHUNK == 0
    reps = SGU_CHUNK // mix_len
    w_tiled = jnp.tile(w_sgu[layer, :, :mix_len, :mix_len], (1, reps, reps))
    bias = jnp.tile(jnp.repeat(b_sgu[layer, :, :mix_len].T, head_dim, axis=1), (reps, 1))

    grid = (batch // n_seg, seq // seg)
    rows = lambda width: pl.BlockSpec((n_seg, seg, width), lambda bi, ti: (bi, ti, 0))
    per_seq = pl.BlockSpec((n_seg, conv_ctx, d_conv), lambda bi, ti: (bi, 0, 0))
    whole = lambda arr: pl.BlockSpec(arr.shape, lambda *_: (0,) * arr.ndim, pipeline_mode=pl.Buffered(1))
    out_specs = [rows(d_model), per_seq]
    out_shape = [jax.ShapeDtypeStruct(x.shape, f32), jax.ShapeDtypeStruct((batch, conv_ctx, d_conv), f32)]
    if emit_v:
        out_specs.append(rows(d_sgu))
        out_shape.append(jax.ShapeDtypeStruct((batch, seq, d_sgu), f32))
    buf_rows = n_seg * (CTX_ROWS + seg)
    return pl.pallas_call(
        functools.partial(_mixer_kernel, alpha=alpha, mix_len=mix_len, carry=grid[1] > 1, emit_v=emit_v),
        grid=grid,
        in_specs=[rows(d_model), per_seq, _layer_block(layer, w_in.shape[1:]), _layer_block(layer, conv_k.shape[1:]),
                  _layer_block(layer, (1, d_conv)), _layer_block(layer, (1, d_conv)), _layer_block(layer, (1, d_conv)),
                  _layer_block(layer, (1, d_sgu)), _layer_block(layer, (1, d_sgu)), whole(w_tiled), whole(bias),
                  _layer_block(layer, w_out.shape[1:]), _layer_block(layer, (1, d_model)),
                  _layer_block(layer, (1, d_model))],
        out_specs=out_specs,
        out_shape=out_shape,
        scratch_shapes=[pltpu.VMEM((buf_rows, d_conv), f32),
                        pltpu.VMEM((SUBLANES - 1, buf_rows, d_conv), f32),
                        pltpu.VMEM((tile, d_sgu), f32),
                        pltpu.VMEM((LANES // head_dim * tile, d_sgu), bf16),
                        pltpu.VMEM((tile, d_mix), bf16)],
        compiler_params=pltpu.CompilerParams(dimension_semantics=("arbitrary", "arbitrary"),
                                             vmem_limit_bytes=VMEM_LIMIT_BYTES),
        name="mixer",
    )(x, ctx, w_in, conv_k, conv_b, conv_g, conv_beta, sgu_g, sgu_beta, w_tiled, bias, w_out, g, b)


def kernel(x_prompt, x_sample, cache_conv, w_ffn1_gate, w_ffn1_up, w_ffn1_down, ln1_g, ln1_b, w_in, conv_k, conv_b,
           conv_ln_g, conv_ln_b, sgu_ln_g, sgu_ln_b, w_sgu, b_sgu, w_out, ln2_g, ln2_b, w_ffn2_gate, w_ffn2_up,
           w_ffn2_down, ln3_g, ln3_b):
    depth = w_in.shape[0]
    alpha = float((2 * depth) ** 0.25)
    batch, seq, d_model = x_prompt.shape
    conv_ctx, d_conv = cache_conv.shape[2:]
    row = lambda p: p[:, None, :]

    wg1, wu1, wd1 = w_ffn1_gate, w_ffn1_up, w_ffn1_down
    wg2, wu2, wd2 = w_ffn2_gate, w_ffn2_up, w_ffn2_down
    w_in_b, w_out_b = w_in.astype(bf16), w_out.astype(bf16)
    ln1, ln2, ln3 = (row(ln1_g), row(ln1_b)), (row(ln2_g), row(ln2_b)), (row(ln3_g), row(ln3_b))
    conv_vecs = (row(conv_b), row(conv_ln_g), row(conv_ln_b))
    sgu_vecs = (row(sgu_ln_g), row(sgu_ln_b))

    def ffn(xp, xs, l, wg, wu, wd, ln):
        yp, ys = _ffn(xp.reshape(-1, d_model), xs.reshape(-1, d_model), l, wg, wu, wd, *ln, alpha=alpha)
        return yp.reshape(xp.shape), ys.reshape(xs.shape)

    def mixer(x, ctx, l, emit_v):
        return _mixer(x, ctx, l, w_in_b, conv_k, *conv_vecs, *sgu_vecs, w_sgu, b_sgu, w_out_b, *ln2,
                      alpha=alpha, emit_v=emit_v)

    zero_ctx = jnp.zeros((batch, conv_ctx, d_conv), f32)
    xp, xs = x_prompt, x_sample
    conv_p, conv_s, sgu_s = [], [], []
    for l in range(depth):
        xp, xs = ffn(xp, xs, l, wg1, wu1, wd1, ln1)
        xp, cp = mixer(xp, zero_ctx, l, emit_v=False)
        xs, cs, vs = mixer(xs, cache_conv[l], l, emit_v=True)
        xp, xs = ffn(xp, xs, l, wg2, wu2, wd2, ln3)
        conv_p.append(cp)
        conv_s.append(cs)
        sgu_s.append(vs)
    return (xp, xs, jnp.stack(conv_p, axis=0), jnp.stack(conv_s, axis=0), jnp.stack(sgu_s, axis=0))
```

```python
import functools

import jax
import jax.numpy as jnp
from jax import lax
from jax.experimental import pallas as pl
from jax.experimental.pallas import tpu as pltpu

N_SGU_HEADS = 8
SGU_CHUNK = 128
LN_EPS = 1e-5

LANES = 128
SUBLANES = 8
BF16_SUBLANES = 16
VMEM_LIMIT_BYTES = 58 * 1024 * 1024

SUB_TILE = 512
FFN_TOKEN_TILE = 512
MIXER_TOKEN_TILE = 512
FFN_HIDDEN_CHUNK = 256
FFN_WEIGHT_PREP_STEPS = 16
CONV_ROW_BLOCK = 64
CTX_ROWS = 32

f32 = jnp.float32
bf16 = jnp.bfloat16


def _layer_norm(y, g, b):
    mu = jnp.mean(y, axis=-1, keepdims=True)
    d = y - mu
    var = jnp.mean(d * d, axis=-1, keepdims=True)
    return d * lax.rsqrt(var + LN_EPS) * g + b


def _gelu(x):
    return 0.5 * x * (1.0 + lax.erf(x * (2.0 ** -0.5)))


def _layer_block(layer, shape):
    zeros = (0,) * len(shape)
    return pl.BlockSpec((None,) + tuple(shape), lambda *_: (layer,) + zeros, pipeline_mode=pl.Buffered(1))


def _ffn_tile(x_ref, wg_ref, wu_ref, wd_ref, g_ref, b_ref, o_ref, h_ref, alpha):
    d_ff = wg_ref.shape[1]
    for r in range(0, x_ref.shape[0], SUB_TILE):
        x = x_ref[pl.ds(r, SUB_TILE), :]
        xb = x.astype(bf16)
        for c in range(0, d_ff, FFN_HIDDEN_CHUNK):
            gate = jnp.dot(xb, wg_ref[:, c:c + FFN_HIDDEN_CHUNK], preferred_element_type=f32)
            up = jnp.dot(xb, wu_ref[:, c:c + FFN_HIDDEN_CHUNK], preferred_element_type=f32)
            h_ref[:, c:c + FFN_HIDDEN_CHUNK] = (jax.nn.silu(gate) * up).astype(bf16)
        f = jnp.dot(h_ref[...], wd_ref[...], preferred_element_type=f32)
        o_ref[pl.ds(r, SUB_TILE), :] = _layer_norm(alpha * x + 0.5 * f, g_ref[...], b_ref[...])


def _ffn_kernel(xp_ref, xs_ref, wg32_ref, wu32_ref, wd32_ref, g_ref, b_ref, op_ref, os_ref, wg_ref, wu_ref, wd_ref,
                h_ref, *, alpha, prep_steps, prompt_tiles):
    i = pl.program_id(0)

    @pl.when(i < prep_steps)
    def _():
        for src, dst in ((wg32_ref, wg_ref), (wu32_ref, wu_ref), (wd32_ref, wd_ref)):
            rows = src.shape[0]
            dst[pl.ds(pl.multiple_of(i * rows, rows), rows), :] = src[...].astype(bf16)

    @pl.when((i >= prep_steps) & (i < prep_steps + prompt_tiles))
    def _():
        _ffn_tile(xp_ref, wg_ref, wu_ref, wd_ref, g_ref, b_ref, op_ref, h_ref, alpha)

    @pl.when(i >= prep_steps + prompt_tiles)
    def _():
        _ffn_tile(xs_ref, wg_ref, wu_ref, wd_ref, g_ref, b_ref, os_ref, h_ref, alpha)


def _ffn(xp, xs, layer, wg, wu, wd, g, b, *, alpha):
    d_model = xp.shape[1]
    d_ff = wg.shape[2]
    tile = FFN_TOKEN_TILE
    prep = FFN_WEIGHT_PREP_STEPS
    assert xp.shape[0] % tile == 0 and xs.shape[0] % tile == 0 and d_ff % FFN_HIDDEN_CHUNK == 0
    assert tile % SUB_TILE == 0 and d_model % (prep * BF16_SUBLANES) == 0 and d_ff % (prep * BF16_SUBLANES) == 0
    prompt_tiles, sample_tiles = xp.shape[0] // tile, xs.shape[0] // tile
    prompt_rows = pl.BlockSpec((tile, d_model), lambda i: (jnp.clip(i - prep, 0, prompt_tiles - 1), 0))
    sample_index = lambda i: (jnp.clip(i - prep - prompt_tiles, 0, sample_tiles - 1), 0)
    sample_in = pl.BlockSpec((tile, d_model), sample_index, pipeline_mode=pl.Buffered(1) if sample_tiles == 1 else None)
    sample_out = pl.BlockSpec((tile, d_model), sample_index)
    slab = lambda rows, cols: pl.BlockSpec((None, rows // prep, cols), lambda i: (layer, jnp.minimum(i, prep - 1), 0))
    return pl.pallas_call(
        functools.partial(_ffn_kernel, alpha=alpha, prep_steps=prep, prompt_tiles=prompt_tiles),
        grid=(prep + prompt_tiles + sample_tiles,),
        in_specs=[prompt_rows, sample_in, slab(d_model, d_ff), slab(d_model, d_ff), slab(d_ff, d_model),
                  _layer_block(layer, (1, d_model)), _layer_block(layer, (1, d_model))],
        out_specs=[prompt_rows, sample_out],
        out_shape=[jax.ShapeDtypeStruct(xp.shape, f32), jax.ShapeDtypeStruct(xs.shape, f32)],
        scratch_shapes=[pltpu.VMEM((d_model, d_ff), bf16), pltpu.VMEM((d_model, d_ff), bf16),
                        pltpu.VMEM((d_ff, d_model), bf16), pltpu.VMEM((SUB_TILE, d_ff), bf16)],
        compiler_params=pltpu.CompilerParams(dimension_semantics=("arbitrary",),
                                             vmem_limit_bytes=VMEM_LIMIT_BYTES),
        name="ffn",
    )(xp, xs, wg, wu, wd, g, b)


def _mixer_kernel(x_ref, ctx_ref, w_in_ref, ck_ref, cb_ref, cg_ref, cbeta_ref, sg_ref, sbeta_ref, ws_ref, bs_ref,
                  w_out_ref, g_ref, b_ref, *rest, alpha, mix_len, carry, emit_v):
    if emit_v:
        y_ref, state_ref, v_ref, buf_ref, shift_ref, u_ref, vb_ref, mix_ref = rest
    else:
        y_ref, state_ref, buf_ref, shift_ref, u_ref, vb_ref, mix_ref = rest
    n_seg, seg, d_model = x_ref.shape
    tile = n_seg * seg
    conv_width, d_conv = ck_ref.shape
    d_sgu = sg_ref.shape[1]
    conv_ctx = conv_width - 1
    pad = CTX_ROWS - conv_ctx
    stride = CTX_ROWS + seg
    sub = min(SUB_TILE, tile)
    piece_rows = min(sub, seg)
    pieces = [[(t, (t // seg) * stride + CTX_ROWS + t % seg, piece_rows) for t in range(t0, t0 + sub, piece_rows)]
              for t0 in range(0, tile, sub)]

    def load_rows(ref, t, rows):
        return ref[t // seg, pl.ds(t % seg, rows), :]

    head_dim = d_sgu // N_SGU_HEADS
    heads_per_group = LANES // head_dim
    lane_head = (lax.broadcasted_iota(jnp.int32, (SGU_CHUNK, d_sgu), 1) // head_dim) % heads_per_group

    @pl.when(pl.program_id(1) == 0)
    def _():
        for s in range(n_seg):
            buf_ref[pl.ds(s * stride, SUBLANES), :] = jnp.zeros((SUBLANES, d_conv), f32)
            buf_ref[pl.ds(s * stride + pad, conv_ctx), :] = ctx_ref[s]

    for sub_pieces in pieces:
        t0 = sub_pieces[0][0]
        xb = jnp.concatenate([load_rows(x_ref, t, n) for t, _, n in sub_pieces], axis=0).astype(bf16)
        a_val = jnp.dot(xb, w_in_ref[:, 0:d_conv], preferred_element_type=f32)
        a_gate = jnp.dot(xb, w_in_ref[:, d_conv:2 * d_conv], preferred_element_type=f32)
        a = a_val * jax.nn.sigmoid(a_gate)
        for i, (t, row, n) in enumerate(sub_pieces):
            buf_ref[pl.ds(row, n), :] = a[i * n:(i + 1) * n]
        z_u = jnp.dot(xb, w_in_ref[:, 2 * d_conv:2 * d_conv + d_sgu], preferred_element_type=f32)
        u_ref[pl.ds(t0, sub), :] = _gelu(z_u)
        z_v = jnp.dot(xb, w_in_ref[:, 2 * d_conv + d_sgu:], preferred_element_type=f32)
        v = _layer_norm(_gelu(z_v), sg_ref[...], sbeta_ref[...])
        for c0 in range(0, sub, SGU_CHUNK):
            for j in range(heads_per_group):
                masked = jnp.where(lane_head == j, v[c0:c0 + SGU_CHUNK], 0.0)
                vb_ref[pl.ds(heads_per_group * (t0 + c0) + j * SGU_CHUNK, SGU_CHUNK), :] = masked.astype(bf16)
        if emit_v:
            for i, (t, row, n) in enumerate(sub_pieces):
                v_ref[t // seg, pl.ds(t % seg, n), :] = v[i * n:(i + 1) * n]
    for s in range(n_seg):
        state_ref[s] = buf_ref[pl.ds((s + 1) * stride - conv_ctx, conv_ctx), :]

    t_idx = lax.broadcasted_iota(jnp.int32, (SGU_CHUNK, SGU_CHUNK), 0)
    s_idx = lax.broadcasted_iota(jnp.int32, (SGU_CHUNK, SGU_CHUNK), 1)
    allowed = (t_idx // mix_len == s_idx // mix_len) & (t_idx >= s_idx)
    w_heads = [jnp.where(allowed, ws_ref[h], 0.0).astype(bf16) for h in range(N_SGU_HEADS)]
    w_groups = [jnp.concatenate(w_heads[p:p + heads_per_group], axis=1)
                for p in range(0, N_SGU_HEADS, heads_per_group)]

    shifted_to = 0
    for sub_pieces in pieces:
        t0 = sub_pieces[0][0]
        shift_end = sub_pieces[-1][1] + sub_pieces[-1][2] - SUBLANES
        for g in range(0, d_conv, LANES):
            cols = slice(g, g + LANES)
            for r0 in range(shifted_to, shift_end, CONV_ROW_BLOCK):
                rows = min(CONV_ROW_BLOCK, shift_end - r0)
                blk = buf_ref[pl.ds(r0, rows + SUBLANES), cols]
                for r in range(1, SUBLANES):
                    shift_ref[r - 1, pl.ds(r0, rows), cols] = pltpu.roll(blk, rows + SUBLANES - r, axis=0)[:rows]
        shifted_to = shift_end
        for t, row, n in sub_pieces:
            for b0 in range(0, n, CONV_ROW_BLOCK):
                rows = min(CONV_ROW_BLOCK, n - b0)
                parts = []
                for g in range(0, d_conv, LANES):
                    cols = slice(g, g + LANES)
                    acc = jnp.broadcast_to(cb_ref[:, cols], (rows, LANES))
                    for k in range(conv_width):
                        q, r = divmod(pad + k, SUBLANES)
                        row0 = row - CTX_ROWS + b0 + q * SUBLANES
                        src = (buf_ref[pl.ds(row0, rows), cols] if r == 0
                               else shift_ref[r - 1, pl.ds(row0, rows), cols])
                        acc = acc + src * ck_ref[pl.ds(k, 1), cols]
                    parts.append(acc)
                c = jax.nn.silu(_layer_norm(jnp.concatenate(parts, axis=1), cg_ref[...], cbeta_ref[...]))
                mix_ref[pl.ds(t + b0, rows), 0:d_conv] = c.astype(bf16)

        for n0 in range(t0, t0 + sub, SGU_CHUNK):
            for p in range(d_sgu // LANES):
                cols = slice(p * LANES, (p + 1) * LANES)
                stacked = vb_ref[pl.ds(heads_per_group * n0, heads_per_group * SGU_CHUNK), cols]
                mixed = jnp.dot(w_groups[p], stacked, preferred_element_type=f32)
                s_out = u_ref[pl.ds(n0, SGU_CHUNK), cols] * (mixed + bs_ref[:, cols])
                mix_ref[pl.ds(n0, SGU_CHUNK), d_conv + p * LANES:d_conv + (p + 1) * LANES] = s_out.astype(bf16)

        m = jnp.dot(mix_ref[pl.ds(t0, sub), :], w_out_ref[...], preferred_element_type=f32)
        x = jnp.concatenate([load_rows(x_ref, t, n) for t, _, n in sub_pieces], axis=0)
        y = _layer_norm(alpha * x + m, g_ref[...], b_ref[...])
        for i, (t, row, n) in enumerate(sub_pieces):
            y_ref[t // seg, pl.ds(t % seg, n), :] = y[i * n:(i + 1) * n]
    if carry:
        buf_ref[pl.ds(0, CTX_ROWS), :] = buf_ref[pl.ds(seg, CTX_ROWS), :]


def _sgu_layout(w_sgu, b_sgu, seq, head_dim):
    mix_len = min(seq, SGU_CHUNK)
    assert SGU_CHUNK % mix_len == 0
    reps = SGU_CHUNK // mix_len
    w_tiled = jnp.tile(w_sgu[:, :, :mix_len, :mix_len], (1, 1, reps, reps))
    bias = jnp.tile(jnp.repeat(jnp.swapaxes(b_sgu[:, :, :mix_len], 1, 2), head_dim, axis=2), (1, reps, 1))
    return w_tiled, bias


def _mixer(x, ctx, layer, w_in, conv_k, conv_b, conv_g, conv_beta, sgu_g, sgu_beta, w_tiled, bias, w_out, g, b, *,
           alpha, emit_v):
    batch, seq, d_model = x.shape
    conv_width, d_conv = conv_k.shape[1:]
    conv_ctx = conv_width - 1
    d_sgu = sgu_g.shape[2]
    d_mix = w_out.shape[1]
    head_dim = d_sgu // N_SGU_HEADS
    tile = MIXER_TOKEN_TILE
    assert d_mix == d_conv + d_sgu and LANES % head_dim == 0 and CTX_ROWS >= conv_ctx
    if seq >= tile:
        assert seq % tile == 0
        n_seg, seg = 1, tile
    else:
        assert tile % seq == 0 and batch % (tile // seq) == 0
        n_seg, seg = tile // seq, seq
    assert seg >= CTX_ROWS and seg % CONV_ROW_BLOCK == 0 and tile % SUB_TILE == 0
    assert seg % SUB_TILE == 0 or SUB_TILE % seg == 0
    mix_len = min(seq, SGU_CHUNK)
    assert SUB_TILE % SGU_CHUNK == 0 and w_tiled.shape[1:] == (N_SGU_HEADS, SGU_CHUNK, SGU_CHUNK)
    assert bias.shape[1:] == (SGU_CHUNK, d_sgu)

    grid = (batch // n_seg, seq // seg)
    rows = lambda width: pl.BlockSpec((n_seg, seg, width), lambda bi, ti: (bi, ti, 0))
    per_seq = pl.BlockSpec((n_seg, conv_ctx, d_conv), lambda bi, ti: (bi, 0, 0))
    out_specs = [rows(d_model), per_seq]
    out_shape = [jax.ShapeDtypeStruct(x.shape, f32), jax.ShapeDtypeStruct((batch, conv_ctx, d_conv), f32)]
    if emit_v:
        out_specs.append(rows(d_sgu))
        out_shape.append(jax.ShapeDtypeStruct((batch, seq, d_sgu), f32))
    buf_rows = n_seg * (CTX_ROWS + seg)
    return pl.pallas_call(
        functools.partial(_mixer_kernel, alpha=alpha, mix_len=mix_len, carry=grid[1] > 1, emit_v=emit_v),
        grid=grid,
        in_specs=[rows(d_model), per_seq, _layer_block(layer, w_in.shape[1:]), _layer_block(layer, conv_k.shape[1:]),
                  _layer_block(layer, (1, d_conv)), _layer_block(layer, (1, d_conv)), _layer_block(layer, (1, d_conv)),
                  _layer_block(layer, (1, d_sgu)), _layer_block(layer, (1, d_sgu)),
                  _layer_block(layer, w_tiled.shape[1:]), _layer_block(layer, bias.shape[1:]),
                  _layer_block(layer, w_out.shape[1:]), _layer_block(layer, (1, d_model)),
                  _layer_block(layer, (1, d_model))],
        out_specs=out_specs,
        out_shape=out_shape,
        scratch_shapes=[pltpu.VMEM((buf_rows, d_conv), f32),
                        pltpu.VMEM((SUBLANES - 1, buf_rows, d_conv), f32),
                        pltpu.VMEM((tile, d_sgu), f32),
                        pltpu.VMEM((LANES // head_dim * tile, d_sgu), bf16),
                        pltpu.VMEM((tile, d_mix), bf16)],
        compiler_params=pltpu.CompilerParams(dimension_semantics=("arbitrary", "arbitrary"),
                                             vmem_limit_bytes=VMEM_LIMIT_BYTES),
        name="mixer",
    )(x, ctx, w_in, conv_k, conv_b, conv_g, conv_beta, sgu_g, sgu_beta, w_tiled, bias, w_out, g, b)


def kernel(x_prompt, x_sample, cache_conv, w_ffn1_gate, w_ffn1_up, w_ffn1_down, ln1_g, ln1_b, w_in, conv_k, conv_b,
           conv_ln_g, conv_ln_b, sgu_ln_g, sgu_ln_b, w_sgu, b_sgu, w_out, ln2_g, ln2_b, w_ffn2_gate, w_ffn2_up,
           w_ffn2_down, ln3_g, ln3_b):
    depth = w_in.shape[0]
    alpha = float((2 * depth) ** 0.25)
    batch, seq, d_model = x_prompt.shape
    conv_ctx, d_conv = cache_conv.shape[2:]
    row = lambda p: p[:, None, :]

    wg1, wu1, wd1 = w_ffn1_gate, w_ffn1_up, w_ffn1_down
    wg2, wu2, wd2 = w_ffn2_gate, w_ffn2_up, w_ffn2_down
    w_in_b, w_out_b = w_in.astype(bf16), w_out.astype(bf16)
    ln1, ln2, ln3 = (row(ln1_g), row(ln1_b)), (row(ln2_g), row(ln2_b)), (row(ln3_g), row(ln3_b))
    conv_vecs = (row(conv_b), row(conv_ln_g), row(conv_ln_b))
    sgu_vecs = (row(sgu_ln_g), row(sgu_ln_b))

    def ffn(xp, xs, l, wg, wu, wd, ln):
        yp, ys = _ffn(xp.reshape(-1, d_model), xs.reshape(-1, d_model), l, wg, wu, wd, *ln, alpha=alpha)
        return yp.reshape(xp.shape), ys.reshape(xs.shape)

    head_dim = sgu_ln_g.shape[1] // N_SGU_HEADS
    sgu_prompt = _sgu_layout(w_sgu, b_sgu, seq, head_dim)
    sgu_sample = _sgu_layout(w_sgu, b_sgu, x_sample.shape[1], head_dim)

    def mixer(x, ctx, l, sgu_layout, emit_v):
        return _mixer(x, ctx, l, w_in_b, conv_k, *conv_vecs, *sgu_vecs, *sgu_layout, w_out_b, *ln2,
                      alpha=alpha, emit_v=emit_v)

    zero_ctx = jnp.zeros((batch, conv_ctx, d_conv), f32)
    xp, xs = x_prompt, x_sample
    conv_p, conv_s, sgu_s = [], [], []
    for l in range(depth):
        xp, xs = ffn(xp, xs, l, wg1, wu1, wd1, ln1)
        xp, cp = mixer(xp, zero_ctx, l, sgu_prompt, emit_v=False)
        xs, cs, vs = mixer(xs, cache_conv[l], l, sgu_sample, emit_v=True)
        xp, xs = ffn(xp, xs, l, wg2, wu2, wd2, ln3)
        conv_p.append(cp)
        conv_s.append(cs)
        sgu_s.append(vs)
    return (xp, xs, jnp.stack(conv_p, axis=0), jnp.stack(conv_s, axis=0), jnp.stack(sgu_s, axis=0))
```

```python
import functools

import jax
import jax.numpy as jnp
from jax import lax
from jax.experimental import pallas as pl
from jax.experimental.pallas import tpu as pltpu

N_SGU_HEADS = 8
SGU_CHUNK = 128
LN_EPS = 1e-5

LANES = 128
SUBLANES = 8
BF16_SUBLANES = 16
VMEM_LIMIT_BYTES = 58 * 1024 * 1024

SUB_TILE = 512
FFN_TOKEN_TILE = 512
MIXER_TOKEN_TILE = 512
FFN_HIDDEN_CHUNK = 256
FFN_WEIGHT_PREP_STEPS = 16
CONV_ROW_BLOCK = 64
CTX_ROWS = 32

f32 = jnp.float32
bf16 = jnp.bfloat16


def _layer_norm(y, g, b):
    mu = jnp.mean(y, axis=-1, keepdims=True)
    d = y - mu
    var = jnp.mean(d * d, axis=-1, keepdims=True)
    return d * lax.rsqrt(var + LN_EPS) * g + b


def _gelu(x):
    return 0.5 * x * (1.0 + lax.erf(x * (2.0 ** -0.5)))


def _layer_block(layer, shape):
    zeros = (0,) * len(shape)
    return pl.BlockSpec((None,) + tuple(shape), lambda *_: (layer,) + zeros, pipeline_mode=pl.Buffered(1))


def _zero_words(v):
    u = pltpu.bitcast(v, jnp.uint32)
    return lax.shift_right_logical(lax.shift_right_logical(u, jnp.uint32(16)), jnp.uint32(16))


def _ffn_step(x_ref, o_ref, wg_ref, wu_ref, wd_ref, g_ref, b_ref, h_ref, s_ref, alpha):
    rows = s_ref.shape[0]
    d_ff = wg_ref.shape[1]
    if x_ref is None:
        o_ref[...] = _layer_norm(s_ref[...], g_ref[...], b_ref[...])
        return
    x = x_ref[...]
    xb = x.astype(bf16)
    chunks = list(range(0, d_ff, FFN_HIDDEN_CHUNK))
    spread = len(chunks) - 1
    bounds = [0] + [rows * c // spread // SUBLANES * SUBLANES for c in range(spread)] + [rows]
    for ci, c in enumerate(chunks):
        lhs = xb
        r0, r1 = bounds[ci], bounds[ci + 1]
        if o_ref is not None and r1 > r0:
            y = _layer_norm(s_ref[pl.ds(r0, r1 - r0), :], g_ref[...], b_ref[...])
            o_ref[pl.ds(r0, r1 - r0), :] = y
            z = _zero_words(y)
            acc = z[0:SUBLANES, 0:LANES]
            for i in range(0, r1 - r0, SUBLANES):
                for j in range(0, z.shape[1], LANES):
                    if i or j:
                        acc = acc | z[i:i + SUBLANES, j:j + LANES]
            head = pltpu.bitcast(pltpu.bitcast(x[0:SUBLANES, 0:LANES], jnp.uint32) | acc, f32)
            x_tied = jnp.concatenate([jnp.concatenate([head, x[0:SUBLANES, LANES:]], axis=1), x[SUBLANES:]], axis=0)
            lhs = x_tied.astype(bf16)
        gate = jnp.dot(lhs, wg_ref[:, c:c + FFN_HIDDEN_CHUNK], preferred_element_type=f32)
        up = jnp.dot(xb, wu_ref[:, c:c + FFN_HIDDEN_CHUNK], preferred_element_type=f32)
        h_ref[:, c:c + FFN_HIDDEN_CHUNK] = (jax.nn.silu(gate) * up).astype(bf16)
    f = jnp.dot(h_ref[...], wd_ref[...], preferred_element_type=f32)
    s_ref[...] = alpha * x + 0.5 * f


def _ffn_kernel(xp_ref, xs_ref, wg32_ref, wu32_ref, wd32_ref, g_ref, b_ref, op_ref, os_ref, wg_ref, wu_ref, wd_ref,
                h_ref, s_ref, *, alpha, prep_steps, prompt_tiles, sample_tiles):
    i = pl.program_id(0)
    j = i - prep_steps
    last = prompt_tiles + sample_tiles
    step = functools.partial(_ffn_step, wg_ref=wg_ref, wu_ref=wu_ref, wd_ref=wd_ref, g_ref=g_ref, b_ref=b_ref,
                             h_ref=h_ref, s_ref=s_ref, alpha=alpha)

    @pl.when(i < prep_steps)
    def _():
        for src, dst in ((wg32_ref, wg_ref), (wu32_ref, wu_ref), (wd32_ref, wd_ref)):
            rows = src.shape[0]
            dst[pl.ds(pl.multiple_of(i * rows, rows), rows), :] = src[...].astype(bf16)

    @pl.when(j == 0)
    def _():
        step(xp_ref, None)

    @pl.when((j >= 1) & (j < prompt_tiles))
    def _():
        step(xp_ref, op_ref)

    @pl.when(j == prompt_tiles)
    def _():
        step(xs_ref, op_ref)

    @pl.when((j > prompt_tiles) & (j < last))
    def _():
        step(xs_ref, os_ref)

    @pl.when(j == last)
    def _():
        step(None, os_ref)


def _ffn(xp, xs, layer, wg, wu, wd, g, b, *, alpha):
    d_model = xp.shape[1]
    d_ff = wg.shape[2]
    tile = FFN_TOKEN_TILE
    prep = FFN_WEIGHT_PREP_STEPS
    assert xp.shape[0] % tile == 0 and xs.shape[0] % tile == 0 and d_ff % FFN_HIDDEN_CHUNK == 0
    assert d_model % (prep * BF16_SUBLANES) == 0 and d_ff % (prep * BF16_SUBLANES) == 0
    prompt_tiles, sample_tiles = xp.shape[0] // tile, xs.shape[0] // tile
    rows_of = lambda first, count: pl.BlockSpec((tile, d_model), lambda i: (jnp.clip(i - first, 0, count - 1), 0))
    slab = lambda rows, cols: pl.BlockSpec((None, rows // prep, cols), lambda i: (layer, jnp.minimum(i, prep - 1), 0))
    return pl.pallas_call(
        functools.partial(_ffn_kernel, alpha=alpha, prep_steps=prep, prompt_tiles=prompt_tiles,
                          sample_tiles=sample_tiles),
        grid=(prep + prompt_tiles + sample_tiles + 1,),
        in_specs=[rows_of(prep, prompt_tiles), rows_of(prep + prompt_tiles, sample_tiles),
                  slab(d_model, d_ff), slab(d_model, d_ff), slab(d_ff, d_model),
                  _layer_block(layer, (1, d_model)), _layer_block(layer, (1, d_model))],
        out_specs=[rows_of(prep + 1, prompt_tiles), rows_of(prep + 1 + prompt_tiles, sample_tiles)],
        out_shape=[jax.ShapeDtypeStruct(xp.shape, f32), jax.ShapeDtypeStruct(xs.shape, f32)],
        scratch_shapes=[pltpu.VMEM((d_model, d_ff), bf16), pltpu.VMEM((d_model, d_ff), bf16),
                        pltpu.VMEM((d_ff, d_model), bf16), pltpu.VMEM((tile, d_ff), bf16),
                        pltpu.VMEM((tile, d_model), f32)],
        compiler_params=pltpu.CompilerParams(dimension_semantics=("arbitrary",),
                                             vmem_limit_bytes=VMEM_LIMIT_BYTES),
        name="ffn",
    )(xp, xs, wg, wu, wd, g, b)


def _mixer_kernel(x_ref, ctx_ref, w_in_ref, ck_ref, cb_ref, cg_ref, cbeta_ref, sg_ref, sbeta_ref, ws_ref, bs_ref,
                  w_out_ref, g_ref, b_ref, *rest, alpha, mix_len, carry, emit_v):
    if emit_v:
        y_ref, state_ref, v_ref, buf_ref, shift_ref, u_ref, vb_ref, mix_ref = rest
    else:
        y_ref, state_ref, buf_ref, shift_ref, u_ref, vb_ref, mix_ref = rest
    n_seg, seg, d_model = x_ref.shape
    tile = n_seg * seg
    conv_width, d_conv = ck_ref.shape
    d_sgu = sg_ref.shape[1]
    conv_ctx = conv_width - 1
    pad = CTX_ROWS - conv_ctx
    stride = CTX_ROWS + seg
    sub = min(SUB_TILE, tile)
    piece_rows = min(sub, seg)
    pieces = [[(t, (t // seg) * stride + CTX_ROWS + t % seg, piece_rows) for t in range(t0, t0 + sub, piece_rows)]
              for t0 in range(0, tile, sub)]

    def load_rows(ref, t, rows):
        return ref[t // seg, pl.ds(t % seg, rows), :]

    head_dim = d_sgu // N_SGU_HEADS
    heads_per_group = LANES // head_dim
    lane_head = (lax.broadcasted_iota(jnp.int32, (SGU_CHUNK, d_sgu), 1) // head_dim) % heads_per_group

    @pl.when(pl.program_id(1) == 0)
    def _():
        for s in range(n_seg):
            buf_ref[pl.ds(s * stride, SUBLANES), :] = jnp.zeros((SUBLANES, d_conv), f32)
            buf_ref[pl.ds(s * stride + pad, conv_ctx), :] = ctx_ref[s]

    for sub_pieces in pieces:
        t0 = sub_pieces[0][0]
        xb = jnp.concatenate([load_rows(x_ref, t, n) for t, _, n in sub_pieces], axis=0).astype(bf16)
        a_val = jnp.dot(xb, w_in_ref[:, 0:d_conv], preferred_element_type=f32)
        a_gate = jnp.dot(xb, w_in_ref[:, d_conv:2 * d_conv], preferred_element_type=f32)
        a = a_val * jax.nn.sigmoid(a_gate)
        for i, (t, row, n) in enumerate(sub_pieces):
            buf_ref[pl.ds(row, n), :] = a[i * n:(i + 1) * n]
        z_u = jnp.dot(xb, w_in_ref[:, 2 * d_conv:2 * d_conv + d_sgu], preferred_element_type=f32)
        u_ref[pl.ds(t0, sub), :] = _gelu(z_u)
        z_v = jnp.dot(xb, w_in_ref[:, 2 * d_conv + d_sgu:], preferred_element_type=f32)
        v = _layer_norm(_gelu(z_v), sg_ref[...], sbeta_ref[...])
        for c0 in range(0, sub, SGU_CHUNK):
            for j in range(heads_per_group):
                masked = jnp.where(lane_head == j, v[c0:c0 + SGU_CHUNK], 0.0)
                vb_ref[pl.ds(heads_per_group * (t0 + c0) + j * SGU_CHUNK, SGU_CHUNK), :] = masked.astype(bf16)
        if emit_v:
            for i, (t, row, n) in enumerate(sub_pieces):
                v_ref[t // seg, pl.ds(t % seg, n), :] = v[i * n:(i + 1) * n]
    for s in range(n_seg):
        state_ref[s] = buf_ref[pl.ds((s + 1) * stride - conv_ctx, conv_ctx), :]

    t_idx = lax.broadcasted_iota(jnp.int32, (SGU_CHUNK, SGU_CHUNK), 0)
    s_idx = lax.broadcasted_iota(jnp.int32, (SGU_CHUNK, SGU_CHUNK), 1)
    allowed = (t_idx // mix_len == s_idx // mix_len) & (t_idx >= s_idx)
    w_heads = [jnp.where(allowed, ws_ref[h], 0.0).astype(bf16) for h in range(N_SGU_HEADS)]
    w_groups = [jnp.concatenate(w_heads[p:p + heads_per_group], axis=1)
                for p in range(0, N_SGU_HEADS, heads_per_group)]

    shifted_to = 0
    for sub_pieces in pieces:
        t0 = sub_pieces[0][0]
        shift_end = sub_pieces[-1][1] + sub_pieces[-1][2] - SUBLANES
        for g in range(0, d_conv, LANES):
            cols = slice(g, g + LANES)
            for r0 in range(shifted_to, shift_end, CONV_ROW_BLOCK):
                rows = min(CONV_ROW_BLOCK, shift_end - r0)
                blk = buf_ref[pl.ds(r0, rows + SUBLANES), cols]
                for r in range(1, SUBLANES):
                    shift_ref[r - 1, pl.ds(r0, rows), cols] = pltpu.roll(blk, rows + SUBLANES - r, axis=0)[:rows]
        shifted_to = shift_end
        for t, row, n in sub_pieces:
            for b0 in range(0, n, CONV_ROW_BLOCK):
                rows = min(CONV_ROW_BLOCK, n - b0)
                parts = []
                for g in range(0, d_conv, LANES):
                    cols = slice(g, g + LANES)
                    acc = jnp.broadcast_to(cb_ref[:, cols], (rows, LANES))
                    for k in range(conv_width):
                        q, r = divmod(pad + k, SUBLANES)
                        row0 = row - CTX_ROWS + b0 + q * SUBLANES
                        src = (buf_ref[pl.ds(row0, rows), cols] if r == 0
                               else shift_ref[r - 1, pl.ds(row0, rows), cols])
                        acc = acc + src * ck_ref[pl.ds(k, 1), cols]
                    parts.append(acc)
                c = jax.nn.silu(_layer_norm(jnp.concatenate(parts, axis=1), cg_ref[...], cbeta_ref[...]))
                mix_ref[pl.ds(t + b0, rows), 0:d_conv] = c.astype(bf16)

        for n0 in range(t0, t0 + sub, SGU_CHUNK):
            for p in range(d_sgu // LANES):
                cols = slice(p * LANES, (p + 1) * LANES)
                stacked = vb_ref[pl.ds(heads_per_group * n0, heads_per_group * SGU_CHUNK), cols]
                mixed = jnp.dot(w_groups[p], stacked, preferred_element_type=f32)
                s_out = u_ref[pl.ds(n0, SGU_CHUNK), cols] * (mixed + bs_ref[:, cols])
                mix_ref[pl.ds(n0, SGU_CHUNK), d_conv + p * LANES:d_conv + (p + 1) * LANES] = s_out.astype(bf16)

        m = jnp.dot(mix_ref[pl.ds(t0, sub), :], w_out_ref[...], preferred_element_type=f32)
        x = jnp.concatenate([load_rows(x_ref, t, n) for t, _, n in sub_pieces], axis=0)
        y = _layer_norm(alpha * x + m, g_ref[...], b_ref[...])
        for i, (t, row, n) in enumerate(sub_pieces):
            y_ref[t // seg, pl.ds(t % seg, n), :] = y[i * n:(i + 1) * n]
    if carry:
        buf_ref[pl.ds(0, CTX_ROWS), :] = buf_ref[pl.ds(seg, CTX_ROWS), :]


def _sgu_layout(w_sgu, b_sgu, seq, head_dim):
    mix_len = min(seq, SGU_CHUNK)
    assert SGU_CHUNK % mix_len == 0
    reps = SGU_CHUNK // mix_len
    w_tiled = jnp.tile(w_sgu[:, :, :mix_len, :mix_len], (1, 1, reps, reps))
    bias = jnp.tile(jnp.repeat(jnp.swapaxes(b_sgu[:, :, :mix_len], 1, 2), head_dim, axis=2), (1, reps, 1))
    return w_tiled, bias


def _mixer(x, ctx, layer, w_in, conv_k, conv_b, conv_g, conv_beta, sgu_g, sgu_beta, w_tiled, bias, w_out, g, b, *,
           alpha, emit_v):
    batch, seq, d_model = x.shape
    conv_width, d_conv = conv_k.shape[1:]
    conv_ctx = conv_width - 1
    d_sgu = sgu_g.shape[2]
    d_mix = w_out.shape[1]
    head_dim = d_sgu // N_SGU_HEADS
    tile = MIXER_TOKEN_TILE
    assert d_mix == d_conv + d_sgu and LANES % head_dim == 0 and CTX_ROWS >= conv_ctx
    if seq >= tile:
        assert seq % tile == 0
        n_seg, seg = 1, tile
    else:
        assert tile % seq == 0 and batch % (tile // seq) == 0
        n_seg, seg = tile // seq, seq
    assert seg >= CTX_ROWS and seg % CONV_ROW_BLOCK == 0 and tile % SUB_TILE == 0
    assert seg % SUB_TILE == 0 or SUB_TILE % seg == 0
    mix_len = min(seq, SGU_CHUNK)
    assert SUB_TILE % SGU_CHUNK == 0 and w_tiled.shape[1:] == (N_SGU_HEADS, SGU_CHUNK, SGU_CHUNK)
    assert bias.shape[1:] == (SGU_CHUNK, d_sgu)

    grid = (batch // n_seg, seq // seg)
    rows = lambda width: pl.BlockSpec((n_seg, seg, width), lambda bi, ti: (bi, ti, 0))
    per_seq = pl.BlockSpec((n_seg, conv_ctx, d_conv), lambda bi, ti: (bi, 0, 0))
    out_specs = [rows(d_model), per_seq]
    out_shape = [jax.ShapeDtypeStruct(x.shape, f32), jax.ShapeDtypeStruct((batch, conv_ctx, d_conv), f32)]
    if emit_v:
        out_specs.append(rows(d_sgu))
        out_shape.append(jax.ShapeDtypeStruct((batch, seq, d_sgu), f32))
    buf_rows = n_seg * (CTX_ROWS + seg)
    return pl.pallas_call(
        functools.partial(_mixer_kernel, alpha=alpha, mix_len=mix_len, carry=grid[1] > 1, emit_v=emit_v),
        grid=grid,
        in_specs=[rows(d_model), per_seq, _layer_block(layer, w_in.shape[1:]), _layer_block(layer, conv_k.shape[1:]),
                  _layer_block(layer, (1, d_conv)), _layer_block(layer, (1, d_conv)), _layer_block(layer, (1, d_conv)),
                  _layer_block(layer, (1, d_sgu)), _layer_block(layer, (1, d_sgu)),
                  _layer_block(layer, w_tiled.shape[1:]), _layer_block(layer, bias.shape[1:]),
                  _layer_block(layer, w_out.shape[1:]), _layer_block(layer, (1, d_model)),
                  _layer_block(layer, (1, d_model))],
        out_specs=out_specs,
        out_shape=out_shape,
        scratch_shapes=[pltpu.VMEM((buf_rows, d_conv), f32),
                        pltpu.VMEM((SUBLANES - 1, buf_rows, d_conv), f32),
                        pltpu.VMEM((tile, d_sgu), f32),
                        pltpu.VMEM((LANES // head_dim * tile, d_sgu), bf16),
                        pltpu.VMEM((tile, d_mix), bf16)],
        compiler_params=pltpu.CompilerParams(dimension_semantics=("arbitrary", "arbitrary"),
                                             vmem_limit_bytes=VMEM_LIMIT_BYTES),
        name="mixer",
    )(x, ctx, w_in, conv_k, conv_b, conv_g, conv_beta, sgu_g, sgu_beta, w_tiled, bias, w_out, g, b)


def kernel(x_prompt, x_sample, cache_conv, w_ffn1_gate, w_ffn1_up, w_ffn1_down, ln1_g, ln1_b, w_in, conv_k, conv_b,
           conv_ln_g, conv_ln_b, sgu_ln_g, sgu_ln_b, w_sgu, b_sgu, w_out, ln2_g, ln2_b, w_ffn2_gate, w_ffn2_up,
           w_ffn2_down, ln3_g, ln3_b):
    depth = w_in.shape[0]
    alpha = float((2 * depth) ** 0.25)
    batch, seq, d_model = x_prompt.shape
    conv_ctx, d_conv = cache_conv.shape[2:]
    row = lambda p: p[:, None, :]

    wg1, wu1, wd1 = w_ffn1_gate, w_ffn1_up, w_ffn1_down
    wg2, wu2, wd2 = w_ffn2_gate, w_ffn2_up, w_ffn2_down
    w_in_b, w_out_b = w_in.astype(bf16), w_out.astype(bf16)
    ln1, ln2, ln3 = (row(ln1_g), row(ln1_b)), (row(ln2_g), row(ln2_b)), (row(ln3_g), row(ln3_b))
    conv_vecs = (row(conv_b), row(conv_ln_g), row(conv_ln_b))
    sgu_vecs = (row(sgu_ln_g), row(sgu_ln_b))

    def ffn(xp, xs, l, wg, wu, wd, ln):
        yp, ys = _ffn(xp.reshape(-1, d_model), xs.reshape(-1, d_model), l, wg, wu, wd, *ln, alpha=alpha)
        return yp.reshape(xp.shape), ys.reshape(xs.shape)

    head_dim = sgu_ln_g.shape[1] // N_SGU_HEADS
    sgu_prompt = _sgu_layout(w_sgu, b_sgu, seq, head_dim)
    sgu_sample = _sgu_layout(w_sgu, b_sgu, x_sample.shape[1], head_dim)

    def mixer(x, ctx, l, sgu_layout, emit_v):
        return _mixer(x, ctx, l, w_in_b, conv_k, *conv_vecs, *sgu_vecs, *sgu_layout, w_out_b, *ln2,
                      alpha=alpha, emit_v=emit_v)

    zero_ctx = jnp.zeros((batch, conv_ctx, d_conv), f32)
    xp, xs = x_prompt, x_sample
    conv_p, conv_s, sgu_s = [], [], []
    for l in range(depth):
        xp, xs = ffn(xp, xs, l, wg1, wu1, wd1, ln1)
        xp, cp = mixer(xp, zero_ctx, l, sgu_prompt, emit_v=False)
        xs, cs, vs = mixer(xs, cache_conv[l], l, sgu_sample, emit_v=True)
        xp, xs = ffn(xp, xs, l, wg2, wu2, wd2, ln3)
        conv_p.append(cp)
        conv_s.append(cs)
        sgu_s.append(vs)
    return (xp, xs, jnp.stack(conv_p, axis=0), jnp.stack(conv_s, axis=0), jnp.stack(sgu_s, axis=0))
```

```python
import functools

import jax
import jax.numpy as jnp
from jax import lax
from jax.experimental import pallas as pl
from jax.experimental.pallas import tpu as pltpu

N_SGU_HEADS = 8
SGU_CHUNK = 128
LN_EPS = 1e-5

LANES = 128
SUBLANES = 8
BF16_SUBLANES = 16
VMEM_LIMIT_BYTES = 58 * 1024 * 1024

SUB_TILE = 512
FFN_TOKEN_TILE = 512
MIXER_TOKEN_TILE = 512
FFN_HIDDEN_CHUNK = 256
FFN_WEIGHT_PREP_STEPS = 16
CONV_ROW_BLOCK = 64
CTX_ROWS = 32

f32 = jnp.float32
bf16 = jnp.bfloat16


def _layer_norm(y, g, b):
    mu = jnp.mean(y, axis=-1, keepdims=True)
    d = y - mu
    var = jnp.mean(d * d, axis=-1, keepdims=True)
    return d * lax.rsqrt(var + LN_EPS) * g + b


def _gelu(x):
    return 0.5 * x * (1.0 + lax.erf(x * (2.0 ** -0.5)))


def _layer_block(layer, shape):
    zeros = (0,) * len(shape)
    return pl.BlockSpec((None,) + tuple(shape), lambda *_: (layer,) + zeros, pipeline_mode=pl.Buffered(1))


def _zero_words(v):
    u = pltpu.bitcast(v, jnp.uint32)
    return lax.shift_right_logical(lax.shift_right_logical(u, jnp.uint32(16)), jnp.uint32(16))


def _ffn_step(x_ref, o_ref, wg_ref, wu_ref, wd_ref, g_ref, b_ref, h_ref, s_ref, alpha):
    rows = s_ref.shape[0]
    d_ff = wg_ref.shape[1]
    if x_ref is None:
        o_ref[...] = _layer_norm(s_ref[...], g_ref[...], b_ref[...])
        return
    x = x_ref[...]
    xb = x.astype(bf16)
    chunks = list(range(0, d_ff, FFN_HIDDEN_CHUNK))
    spread = len(chunks) - 1
    bounds = [0] + [rows * c // spread // SUBLANES * SUBLANES for c in range(spread)] + [rows]
    for ci, c in enumerate(chunks):
        lhs = xb
        r0, r1 = bounds[ci], bounds[ci + 1]
        if o_ref is not None and r1 > r0:
            y = _layer_norm(s_ref[pl.ds(r0, r1 - r0), :], g_ref[...], b_ref[...])
            o_ref[pl.ds(r0, r1 - r0), :] = y
            z = _zero_words(y)
            acc = z[0:SUBLANES, 0:LANES]
            for i in range(0, r1 - r0, SUBLANES):
                for j in range(0, z.shape[1], LANES):
                    if i or j:
                        acc = acc | z[i:i + SUBLANES, j:j + LANES]
            head = pltpu.bitcast(pltpu.bitcast(x[0:SUBLANES, 0:LANES], jnp.uint32) | acc, f32)
            x_tied = jnp.concatenate([jnp.concatenate([head, x[0:SUBLANES, LANES:]], axis=1), x[SUBLANES:]], axis=0)
            lhs = x_tied.astype(bf16)
        gate = jnp.dot(lhs, wg_ref[:, c:c + FFN_HIDDEN_CHUNK], preferred_element_type=f32)
        up = jnp.dot(xb, wu_ref[:, c:c + FFN_HIDDEN_CHUNK], preferred_element_type=f32)
        h_ref[:, c:c + FFN_HIDDEN_CHUNK] = (jax.nn.silu(gate) * up).astype(bf16)
    f = jnp.dot(h_ref[...], wd_ref[...], preferred_element_type=f32)
    s_ref[...] = alpha * x + 0.5 * f


def _ffn_kernel(xp_ref, xs_ref, wg32_ref, wu32_ref, wd32_ref, g_ref, b_ref, op_ref, os_ref, wg_ref, wu_ref, wd_ref,
                h_ref, s_ref, *, alpha, prep_steps, prompt_tiles, sample_tiles):
    i = pl.program_id(0)
    j = i - prep_steps
    last = prompt_tiles + sample_tiles
    step = functools.partial(_ffn_step, wg_ref=wg_ref, wu_ref=wu_ref, wd_ref=wd_ref, g_ref=g_ref, b_ref=b_ref,
                             h_ref=h_ref, s_ref=s_ref, alpha=alpha)

    @pl.when(i < prep_steps)
    def _():
        for src, dst in ((wg32_ref, wg_ref), (wu32_ref, wu_ref), (wd32_ref, wd_ref)):
            rows = src.shape[0]
            dst[pl.ds(pl.multiple_of(i * rows, rows), rows), :] = src[...].astype(bf16)

    @pl.when(j == 0)
    def _():
        step(xp_ref, None)

    @pl.when((j >= 1) & (j < prompt_tiles))
    def _():
        step(xp_ref, op_ref)

    @pl.when(j == prompt_tiles)
    def _():
        step(xs_ref, op_ref)

    @pl.when((j > prompt_tiles) & (j < last))
    def _():
        step(xs_ref, os_ref)

    @pl.when(j == last)
    def _():
        step(None, os_ref)


def _ffn(xp, xs, layer, wg, wu, wd, g, b, *, alpha):
    d_model = xp.shape[1]
    d_ff = wg.shape[2]
    tile = FFN_TOKEN_TILE
    prep = FFN_WEIGHT_PREP_STEPS
    assert xp.shape[0] % tile == 0 and xs.shape[0] % tile == 0 and d_ff % FFN_HIDDEN_CHUNK == 0
    assert d_model % (prep * BF16_SUBLANES) == 0 and d_ff % (prep * BF16_SUBLANES) == 0
    prompt_tiles, sample_tiles = xp.shape[0] // tile, xs.shape[0] // tile
    rows_of = lambda first, count: pl.BlockSpec((tile, d_model), lambda i: (jnp.clip(i - first, 0, count - 1), 0))
    slab = lambda rows, cols: pl.BlockSpec((None, rows // prep, cols), lambda i: (layer, jnp.minimum(i, prep - 1), 0))
    return pl.pallas_call(
        functools.partial(_ffn_kernel, alpha=alpha, prep_steps=prep, prompt_tiles=prompt_tiles,
                          sample_tiles=sample_tiles),
        grid=(prep + prompt_tiles + sample_tiles + 1,),
        in_specs=[rows_of(prep, prompt_tiles), rows_of(prep + prompt_tiles, sample_tiles),
                  slab(d_model, d_ff), slab(d_model, d_ff), slab(d_ff, d_model),
                  _layer_block(layer, (1, d_model)), _layer_block(layer, (1, d_model))],
        out_specs=[rows_of(prep + 1, prompt_tiles), rows_of(prep + 1 + prompt_tiles, sample_tiles)],
        out_shape=[jax.ShapeDtypeStruct(xp.shape, f32), jax.ShapeDtypeStruct(xs.shape, f32)],
        scratch_shapes=[pltpu.VMEM((d_model, d_ff), bf16), pltpu.VMEM((d_model, d_ff), bf16),
                        pltpu.VMEM((d_ff, d_model), bf16), pltpu.VMEM((tile, d_ff), bf16),
                        pltpu.VMEM((tile, d_model), f32)],
        compiler_params=pltpu.CompilerParams(dimension_semantics=("arbitrary",),
                                             vmem_limit_bytes=VMEM_LIMIT_BYTES),
        name="ffn",
    )(xp, xs, wg, wu, wd, g, b)


def _mixer_kernel(x_ref, ctx_ref, w_in_ref, ck_ref, cb_ref, cg_ref, cbeta_ref, sg_ref, sbeta_ref, ws_ref, bs_ref,
                  w_out_ref, g_ref, b_ref, *rest, alpha, mix_len, carry, emit_v):
    if emit_v:
        y_ref, state_ref, v_ref, buf_ref, shift_ref, u_ref, vb_ref, mix_ref = rest
    else:
        y_ref, state_ref, buf_ref, shift_ref, u_ref, vb_ref, mix_ref = rest
    n_seg, seg, d_model = x_ref.shape
    tile = n_seg * seg
    conv_width, d_conv = ck_ref.shape
    d_sgu = sg_ref.shape[1]
    conv_ctx = conv_width - 1
    pad = CTX_ROWS - conv_ctx
    stride = CTX_ROWS + seg
    sub = min(SUB_TILE, tile)
    piece_rows = min(sub, seg)
    pieces = [[(t, (t // seg) * stride + CTX_ROWS + t % seg, piece_rows) for t in range(t0, t0 + sub, piece_rows)]
              for t0 in range(0, tile, sub)]

    def load_rows(ref, t, rows):
        return ref[t // seg, pl.ds(t % seg, rows), :]

    head_dim = d_sgu // N_SGU_HEADS
    heads_per_group = LANES // head_dim
    lane_head = (lax.broadcasted_iota(jnp.int32, (SGU_CHUNK, d_sgu), 1) // head_dim) % heads_per_group

    @pl.when(pl.program_id(1) == 0)
    def _():
        for s in range(n_seg):
            buf_ref[pl.ds(s * stride, SUBLANES), 0:d_conv] = jnp.zeros((SUBLANES, d_conv), f32)
            buf_ref[pl.ds(s * stride + pad, conv_ctx), 0:d_conv] = ctx_ref[s]

    for sub_pieces in pieces:
        t0 = sub_pieces[0][0]
        xb = jnp.concatenate([load_rows(x_ref, t, n) for t, _, n in sub_pieces], axis=0).astype(bf16)
        a_val = jnp.dot(xb, w_in_ref[:, 0:d_conv], preferred_element_type=f32)
        a_gate = jnp.dot(xb, w_in_ref[:, d_conv:2 * d_conv], preferred_element_type=f32)
        a = a_val * jax.nn.sigmoid(a_gate)
        for i, (t, row, n) in enumerate(sub_pieces):
            buf_ref[pl.ds(row, n), 0:d_conv] = a[i * n:(i + 1) * n]
        z_u = jnp.dot(xb, w_in_ref[:, 2 * d_conv:2 * d_conv + d_sgu], preferred_element_type=f32)
        u_ref[pl.ds(t0, sub), :] = _gelu(z_u)
        z_v = jnp.dot(xb, w_in_ref[:, 2 * d_conv + d_sgu:], preferred_element_type=f32)
        v = _layer_norm(_gelu(z_v), sg_ref[...], sbeta_ref[...])
        for c0 in range(0, sub, SGU_CHUNK):
            for j in range(heads_per_group):
                masked = jnp.where(lane_head == j, v[c0:c0 + SGU_CHUNK], 0.0)
                vb_ref[pl.ds(heads_per_group * (t0 + c0) + j * SGU_CHUNK, SGU_CHUNK), :] = masked.astype(bf16)
        if emit_v:
            for i, (t, row, n) in enumerate(sub_pieces):
                v_ref[t // seg, pl.ds(t % seg, n), :] = v[i * n:(i + 1) * n]
    for s in range(n_seg):
        state_ref[s] = buf_ref[pl.ds((s + 1) * stride - conv_ctx, conv_ctx), 0:d_conv]

    t_idx = lax.broadcasted_iota(jnp.int32, (SGU_CHUNK, SGU_CHUNK), 0)
    s_idx = lax.broadcasted_iota(jnp.int32, (SGU_CHUNK, SGU_CHUNK), 1)
    allowed = (t_idx // mix_len == s_idx // mix_len) & (t_idx >= s_idx)
    w_heads = [jnp.where(allowed, ws_ref[h], 0.0).astype(bf16) for h in range(N_SGU_HEADS)]
    w_groups = [jnp.concatenate(w_heads[p:p + heads_per_group], axis=1)
                for p in range(0, N_SGU_HEADS, heads_per_group)]

    shifted_to = 0
    for sub_pieces in pieces:
        t0 = sub_pieces[0][0]
        shift_end = sub_pieces[-1][1] + sub_pieces[-1][2] - SUBLANES
        for g in range(0, d_conv, LANES):
            cols = slice(g, g + LANES)
            for r0 in range(shifted_to, shift_end, CONV_ROW_BLOCK):
                rows = min(CONV_ROW_BLOCK, shift_end - r0)
                blk = buf_ref[pl.ds(r0, rows + SUBLANES), cols]
                for r in range(1, SUBLANES):
                    shift_ref[r - 1, pl.ds(r0, rows), cols] = pltpu.roll(blk, rows + SUBLANES - r, axis=0)[:rows]
        shifted_to = shift_end
        for t, row, n in sub_pieces:
            for b0 in range(0, n, CONV_ROW_BLOCK):
                rows = min(CONV_ROW_BLOCK, n - b0)
                parts = []
                for g in range(0, d_conv, LANES):
                    cols = slice(g, g + LANES)
                    acc = jnp.broadcast_to(cb_ref[:, cols], (rows, LANES))
                    for k in range(conv_width):
                        q, r = divmod(pad + k, SUBLANES)
                        row0 = row - CTX_ROWS + b0 + q * SUBLANES
                        src = (buf_ref[pl.ds(row0, rows), cols] if r == 0
                               else shift_ref[r - 1, pl.ds(row0, rows), cols])
                        acc = acc + src * ck_ref[pl.ds(k, 1), cols]
                    parts.append(acc)
                c = jax.nn.silu(_layer_norm(jnp.concatenate(parts, axis=1), cg_ref[...], cbeta_ref[...]))
                mix_ref[pl.ds(t + b0, rows), 0:d_conv] = c.astype(bf16)

        for n0 in range(t0, t0 + sub, SGU_CHUNK):
            for p in range(d_sgu // LANES):
                cols = slice(p * LANES, (p + 1) * LANES)
                stacked = vb_ref[pl.ds(heads_per_group * n0, heads_per_group * SGU_CHUNK), cols]
                mixed = jnp.dot(w_groups[p], stacked, preferred_element_type=f32)
                s_out = u_ref[pl.ds(n0, SGU_CHUNK), cols] * (mixed + bs_ref[:, cols])
                mix_ref[pl.ds(n0, SGU_CHUNK), d_conv + p * LANES:d_conv + (p + 1) * LANES] = s_out.astype(bf16)

        m = jnp.dot(mix_ref[pl.ds(t0, sub), :], w_out_ref[...], preferred_element_type=f32)
        x = jnp.concatenate([load_rows(x_ref, t, n) for t, _, n in sub_pieces], axis=0)
        y = _layer_norm(alpha * x + m, g_ref[...], b_ref[...])
        for i, (t, row, n) in enumerate(sub_pieces):
            y_ref[t // seg, pl.ds(t % seg, n), :] = y[i * n:(i + 1) * n]
    if carry:
        buf_ref[pl.ds(0, CTX_ROWS), 0:d_conv] = buf_ref[pl.ds(seg, CTX_ROWS), 0:d_conv]


def _sgu_layout(w_sgu, b_sgu, seq, head_dim):
    mix_len = min(seq, SGU_CHUNK)
    assert SGU_CHUNK % mix_len == 0
    reps = SGU_CHUNK // mix_len
    w_tiled = jnp.tile(w_sgu[:, :, :mix_len, :mix_len], (1, 1, reps, reps))
    bias = jnp.tile(jnp.repeat(jnp.swapaxes(b_sgu[:, :, :mix_len], 1, 2), head_dim, axis=2), (1, reps, 1))
    return w_tiled, bias


def _mixer(x, ctx, layer, w_in, conv_k, conv_b, conv_g, conv_beta, sgu_g, sgu_beta, w_tiled, bias, w_out, g, b, *,
           alpha, emit_v):
    batch, seq, d_model = x.shape
    conv_width, d_conv = conv_k.shape[1:]
    conv_ctx = conv_width - 1
    d_sgu = sgu_g.shape[2]
    d_mix = w_out.shape[1]
    head_dim = d_sgu // N_SGU_HEADS
    tile = MIXER_TOKEN_TILE
    assert d_mix == d_conv + d_sgu and LANES % head_dim == 0 and CTX_ROWS >= conv_ctx
    if seq >= tile:
        assert seq % tile == 0
        n_seg, seg = 1, tile
    else:
        assert tile % seq == 0 and batch % (tile // seq) == 0
        n_seg, seg = tile // seq, seq
    assert seg >= CTX_ROWS and seg % CONV_ROW_BLOCK == 0 and tile % SUB_TILE == 0
    assert seg % SUB_TILE == 0 or SUB_TILE % seg == 0
    mix_len = min(seq, SGU_CHUNK)
    assert SUB_TILE % SGU_CHUNK == 0 and w_tiled.shape[1:] == (N_SGU_HEADS, SGU_CHUNK, SGU_CHUNK)
    assert bias.shape[1:] == (SGU_CHUNK, d_sgu)

    grid = (batch // n_seg, seq // seg)
    rows = lambda width: pl.BlockSpec((n_seg, seg, width), lambda bi, ti: (bi, ti, 0))
    per_seq = pl.BlockSpec((n_seg, conv_ctx, d_conv), lambda bi, ti: (bi, 0, 0))
    out_specs = [rows(d_model), per_seq]
    out_shape = [jax.ShapeDtypeStruct(x.shape, f32), jax.ShapeDtypeStruct((batch, conv_ctx, d_conv), f32)]
    if emit_v:
        out_specs.append(rows(d_sgu))
        out_shape.append(jax.ShapeDtypeStruct((batch, seq, d_sgu), f32))
    buf_rows = n_seg * (CTX_ROWS + seg)
    return pl.pallas_call(
        functools.partial(_mixer_kernel, alpha=alpha, mix_len=mix_len, carry=grid[1] > 1, emit_v=emit_v),
        grid=grid,
        in_specs=[rows(d_model), per_seq, _layer_block(layer, w_in.shape[1:]), _layer_block(layer, conv_k.shape[1:]),
                  _layer_block(layer, (1, d_conv)), _layer_block(layer, (1, d_conv)), _layer_block(layer, (1, d_conv)),
                  _layer_block(layer, (1, d_sgu)), _layer_block(layer, (1, d_sgu)),
                  _layer_block(layer, w_tiled.shape[1:]), _layer_block(layer, bias.shape[1:]),
                  _layer_block(layer, w_out.shape[1:]), _layer_block(layer, (1, d_model)),
                  _layer_block(layer, (1, d_model))],
        out_specs=out_specs,
        out_shape=out_shape,
        scratch_shapes=[pltpu.VMEM((buf_rows + SUBLANES, d_conv + LANES), f32),
                        pltpu.VMEM((SUBLANES - 1, buf_rows + SUBLANES, d_conv + LANES), f32),
                        pltpu.VMEM((tile, d_sgu), f32),
                        pltpu.VMEM((LANES // head_dim * tile, d_sgu), bf16),
                        pltpu.VMEM((tile, d_mix), bf16)],
        compiler_params=pltpu.CompilerParams(dimension_semantics=("arbitrary", "arbitrary"),
                                             vmem_limit_bytes=VMEM_LIMIT_BYTES),
        name="mixer",
    )(x, ctx, w_in, conv_k, conv_b, conv_g, conv_beta, sgu_g, sgu_beta, w_tiled, bias, w_out, g, b)


def kernel(x_prompt, x_sample, cache_conv, w_ffn1_gate, w_ffn1_up, w_ffn1_down, ln1_g, ln1_b, w_in, conv_k, conv_b,
           conv_ln_g, conv_ln_b, sgu_ln_g, sgu_ln_b, w_sgu, b_sgu, w_out, ln2_g, ln2_b, w_ffn2_gate, w_ffn2_up,
           w_ffn2_down, ln3_g, ln3_b):
    depth = w_in.shape[0]
    alpha = float((2 * depth) ** 0.25)
    batch, seq, d_model = x_prompt.shape
    conv_ctx, d_conv = cache_conv.shape[2:]
    row = lambda p: p[:, None, :]

    wg1, wu1, wd1 = w_ffn1_gate, w_ffn1_up, w_ffn1_down
    wg2, wu2, wd2 = w_ffn2_gate, w_ffn2_up, w_ffn2_down
    w_in_b, w_out_b = w_in.astype(bf16), w_out.astype(bf16)
    ln1, ln2, ln3 = (row(ln1_g), row(ln1_b)), (row(ln2_g), row(ln2_b)), (row(ln3_g), row(ln3_b))
    conv_vecs = (row(conv_b), row(conv_ln_g), row(conv_ln_b))
    sgu_vecs = (row(sgu_ln_g), row(sgu_ln_b))

    def ffn(xp, xs, l, wg, wu, wd, ln):
        yp, ys = _ffn(xp.reshape(-1, d_model), xs.reshape(-1, d_model), l, wg, wu, wd, *ln, alpha=alpha)
        return yp.reshape(xp.shape), ys.reshape(xs.shape)

    head_dim = sgu_ln_g.shape[1] // N_SGU_HEADS
    sgu_prompt = _sgu_layout(w_sgu, b_sgu, seq, head_dim)
    sgu_sample = _sgu_layout(w_sgu, b_sgu, x_sample.shape[1], head_dim)

    def mixer(x, ctx, l, sgu_layout, emit_v):
        return _mixer(x, ctx, l, w_in_b, conv_k, *conv_vecs, *sgu_vecs, *sgu_layout, w_out_b, *ln2,
                      alpha=alpha, emit_v=emit_v)

    zero_ctx = jnp.zeros((batch, conv_ctx, d_conv), f32)
    xp, xs = x_prompt, x_sample
    conv_p, conv_s, sgu_s = [], [], []
    for l in range(depth):
        xp, xs = ffn(xp, xs, l, wg1, wu1, wd1, ln1)
        xp, cp = mixer(xp, zero_ctx, l, sgu_prompt, emit_v=False)
        xs, cs, vs = mixer(xs, cache_conv[l], l, sgu_sample, emit_v=True)
        xp, xs = ffn(xp, xs, l, wg2, wu2, wd2, ln3)
        conv_p.append(cp)
        conv_s.append(cs)
        sgu_s.append(vs)
    return (xp, xs, jnp.stack(conv_p, axis=0), jnp.stack(conv_s, axis=0), jnp.stack(sgu_s, axis=0))
```

```python
import functools

import jax
import jax.numpy as jnp
from jax import lax
from jax.experimental import pallas as pl
from jax.experimental.pallas import tpu as pltpu

N_SGU_HEADS = 8
SGU_CHUNK = 128
LN_EPS = 1e-5

LANES = 128
SUBLANES = 8
BF16_SUBLANES = 16
VMEM_LIMIT_BYTES = 58 * 1024 * 1024

SUB_TILE = 512
FFN_TOKEN_TILE = 512
MIXER_TOKEN_TILE = 512
FFN_HIDDEN_CHUNK = 256
FFN_WEIGHT_PREP_STEPS = 16
CONV_ROW_BLOCK = 64
CTX_ROWS = 32

f32 = jnp.float32
bf16 = jnp.bfloat16


def _layer_norm(y, g, b):
    mu = jnp.mean(y, axis=-1, keepdims=True)
    d = y - mu
    var = jnp.mean(d * d, axis=-1, keepdims=True)
    return d * lax.rsqrt(var + LN_EPS) * g + b


def _gelu(x):
    return 0.5 * x * (1.0 + lax.erf(x * (2.0 ** -0.5)))


def _layer_block(layer, shape):
    zeros = (0,) * len(shape)
    return pl.BlockSpec((None,) + tuple(shape), lambda *_: (layer,) + zeros, pipeline_mode=pl.Buffered(1))


def _zero_words(v):
    u = pltpu.bitcast(v, jnp.uint32)
    return lax.shift_right_logical(lax.shift_right_logical(u, jnp.uint32(16)), jnp.uint32(16))


def _ffn_step(x_ref, o_ref, wg_ref, wu_ref, wd_ref, g_ref, b_ref, h_ref, s_ref, alpha):
    rows = s_ref.shape[0]
    d_ff = wg_ref.shape[1]
    if x_ref is None:
        o_ref[...] = _layer_norm(s_ref[...], g_ref[...], b_ref[...])
        return
    x = x_ref[...]
    xb = x.astype(bf16)
    chunks = list(range(0, d_ff, FFN_HIDDEN_CHUNK))
    spread = len(chunks) - 1
    bounds = [0] + [rows * c // spread // SUBLANES * SUBLANES for c in range(spread)] + [rows]
    for ci, c in enumerate(chunks):
        lhs = xb
        r0, r1 = bounds[ci], bounds[ci + 1]
        if o_ref is not None and r1 > r0:
            y = _layer_norm(s_ref[pl.ds(r0, r1 - r0), :], g_ref[...], b_ref[...])
            o_ref[pl.ds(r0, r1 - r0), :] = y
            z = _zero_words(y)
            acc = z[0:SUBLANES, 0:LANES]
            for i in range(0, r1 - r0, SUBLANES):
                for j in range(0, z.shape[1], LANES):
                    if i or j:
                        acc = acc | z[i:i + SUBLANES, j:j + LANES]
            head = pltpu.bitcast(pltpu.bitcast(x[0:SUBLANES, 0:LANES], jnp.uint32) | acc, f32)
            x_tied = jnp.concatenate([jnp.concatenate([head, x[0:SUBLANES, LANES:]], axis=1), x[SUBLANES:]], axis=0)
            lhs = x_tied.astype(bf16)
        gate = jnp.dot(lhs, wg_ref[:, c:c + FFN_HIDDEN_CHUNK], preferred_element_type=f32)
        up = jnp.dot(xb, wu_ref[:, c:c + FFN_HIDDEN_CHUNK], preferred_element_type=f32)
        h_ref[:, c:c + FFN_HIDDEN_CHUNK] = (jax.nn.silu(gate) * up).astype(bf16)
    f = jnp.dot(h_ref[...], wd_ref[...], preferred_element_type=f32)
    s_ref[...] = alpha * x + 0.5 * f


def _ffn_kernel(xp_ref, xs_ref, wg32_ref, wu32_ref, wd32_ref, g_ref, b_ref, op_ref, os_ref, wg_ref, wu_ref, wd_ref,
                h_ref, s_ref, *, alpha, prep_steps, prompt_tiles, sample_tiles):
    i = pl.program_id(0)
    j = i - prep_steps
    last = prompt_tiles + sample_tiles
    step = functools.partial(_ffn_step, wg_ref=wg_ref, wu_ref=wu_ref, wd_ref=wd_ref, g_ref=g_ref, b_ref=b_ref,
                             h_ref=h_ref, s_ref=s_ref, alpha=alpha)

    @pl.when(i < prep_steps)
    def _():
        for src, dst in ((wg32_ref, wg_ref), (wu32_ref, wu_ref), (wd32_ref, wd_ref)):
            rows = src.shape[0]
            dst[pl.ds(pl.multiple_of(i * rows, rows), rows), :] = src[...].astype(bf16)

    @pl.when(j == -1)
    def _():
        s_ref[...] = jnp.zeros(s_ref.shape, f32)

    @pl.when((j >= 0) & (j < prompt_tiles))
    def _():
        step(xp_ref, op_ref)

    @pl.when(j == prompt_tiles)
    def _():
        step(xs_ref, op_ref)

    @pl.when((j > prompt_tiles) & (j < last))
    def _():
        step(xs_ref, os_ref)

    @pl.when(j == last)
    def _():
        step(None, os_ref)


def _ffn(xp, xs, layer, wg, wu, wd, g, b, *, alpha):
    d_model = xp.shape[1]
    d_ff = wg.shape[2]
    tile = FFN_TOKEN_TILE
    prep = FFN_WEIGHT_PREP_STEPS
    assert xp.shape[0] % tile == 0 and xs.shape[0] % tile == 0 and d_ff % FFN_HIDDEN_CHUNK == 0
    assert d_model % (prep * BF16_SUBLANES) == 0 and d_ff % (prep * BF16_SUBLANES) == 0
    prompt_tiles, sample_tiles = xp.shape[0] // tile, xs.shape[0] // tile
    rows_of = lambda first, count: pl.BlockSpec((tile, d_model), lambda i: (jnp.clip(i - first, 0, count - 1), 0))
    slab = lambda rows, cols: pl.BlockSpec((None, rows // prep, cols), lambda i: (layer, jnp.minimum(i, prep - 1), 0))
    return pl.pallas_call(
        functools.partial(_ffn_kernel, alpha=alpha, prep_steps=prep, prompt_tiles=prompt_tiles,
                          sample_tiles=sample_tiles),
        grid=(prep + prompt_tiles + sample_tiles + 1,),
        in_specs=[rows_of(prep, prompt_tiles), rows_of(prep + prompt_tiles, sample_tiles),
                  slab(d_model, d_ff), slab(d_model, d_ff), slab(d_ff, d_model),
                  _layer_block(layer, (1, d_model)), _layer_block(layer, (1, d_model))],
        out_specs=[rows_of(prep + 1, prompt_tiles), rows_of(prep + 1 + prompt_tiles, sample_tiles)],
        out_shape=[jax.ShapeDtypeStruct(xp.shape, f32), jax.ShapeDtypeStruct(xs.shape, f32)],
        scratch_shapes=[pltpu.VMEM((d_model, d_ff), bf16), pltpu.VMEM((d_model, d_ff), bf16),
                        pltpu.VMEM((d_ff, d_model), bf16), pltpu.VMEM((tile, d_ff), bf16),
                        pltpu.VMEM((tile, d_model), f32)],
        compiler_params=pltpu.CompilerParams(dimension_semantics=("arbitrary",),
                                             vmem_limit_bytes=VMEM_LIMIT_BYTES),
        name="ffn",
    )(xp, xs, wg, wu, wd, g, b)


def _mixer_kernel(x_ref, ctx_ref, w_in_ref, ck_ref, cb_ref, cg_ref, cbeta_ref, sg_ref, sbeta_ref, ws_ref, bs_ref,
                  w_out_ref, g_ref, b_ref, *rest, alpha, mix_len, carry, emit_v):
    if emit_v:
        y_ref, state_ref, v_ref, buf_ref, shift_ref, u_ref, vb_ref, mix_ref = rest
    else:
        y_ref, state_ref, buf_ref, shift_ref, u_ref, vb_ref, mix_ref = rest
    n_seg, seg, d_model = x_ref.shape
    tile = n_seg * seg
    conv_width, d_conv = ck_ref.shape
    d_sgu = sg_ref.shape[1]
    conv_ctx = conv_width - 1
    pad = CTX_ROWS - conv_ctx
    stride = CTX_ROWS + seg
    sub = min(SUB_TILE, tile)
    piece_rows = min(sub, seg)
    pieces = [[(t, (t // seg) * stride + CTX_ROWS + t % seg, piece_rows) for t in range(t0, t0 + sub, piece_rows)]
              for t0 in range(0, tile, sub)]

    def load_rows(ref, t, rows):
        return ref[t // seg, pl.ds(t % seg, rows), :]

    head_dim = d_sgu // N_SGU_HEADS
    heads_per_group = LANES // head_dim
    lane_head = (lax.broadcasted_iota(jnp.int32, (SGU_CHUNK, d_sgu), 1) // head_dim) % heads_per_group

    @pl.when(pl.program_id(1) == 0)
    def _():
        for s in range(n_seg):
            buf_ref[pl.ds(s * stride, SUBLANES), :] = jnp.zeros((SUBLANES, d_conv), f32)
            buf_ref[pl.ds(s * stride + pad, conv_ctx), :] = ctx_ref[s]

    for sub_pieces in pieces:
        t0 = sub_pieces[0][0]
        xb = jnp.concatenate([load_rows(x_ref, t, n) for t, _, n in sub_pieces], axis=0).astype(bf16)
        a_val = jnp.dot(xb, w_in_ref[:, 0:d_conv], preferred_element_type=f32)
        a_gate = jnp.dot(xb, w_in_ref[:, d_conv:2 * d_conv], preferred_element_type=f32)
        a = a_val * jax.nn.sigmoid(a_gate)
        for i, (t, row, n) in enumerate(sub_pieces):
            buf_ref[pl.ds(row, n), :] = a[i * n:(i + 1) * n]
        z_u = jnp.dot(xb, w_in_ref[:, 2 * d_conv:2 * d_conv + d_sgu], preferred_element_type=f32)
        u_ref[pl.ds(t0, sub), :] = _gelu(z_u)
        z_v = jnp.dot(xb, w_in_ref[:, 2 * d_conv + d_sgu:], preferred_element_type=f32)
        v = _layer_norm(_gelu(z_v), sg_ref[...], sbeta_ref[...])
        for c0 in range(0, sub, SGU_CHUNK):
            for j in range(heads_per_group):
                masked = jnp.where(lane_head == j, v[c0:c0 + SGU_CHUNK], 0.0)
                vb_ref[pl.ds(heads_per_group * (t0 + c0) + j * SGU_CHUNK, SGU_CHUNK), :] = masked.astype(bf16)
        if emit_v:
            for i, (t, row, n) in enumerate(sub_pieces):
                v_ref[t // seg, pl.ds(t % seg, n), :] = v[i * n:(i + 1) * n]
    for s in range(n_seg):
        state_ref[s] = buf_ref[pl.ds((s + 1) * stride - conv_ctx, conv_ctx), :]

    t_idx = lax.broadcasted_iota(jnp.int32, (SGU_CHUNK, SGU_CHUNK), 0)
    s_idx = lax.broadcasted_iota(jnp.int32, (SGU_CHUNK, SGU_CHUNK), 1)
    allowed = (t_idx // mix_len == s_idx // mix_len) & (t_idx >= s_idx)
    w_heads = [jnp.where(allowed, ws_ref[h], 0.0).astype(bf16) for h in range(N_SGU_HEADS)]
    w_groups = [jnp.concatenate(w_heads[p:p + heads_per_group], axis=1)
                for p in range(0, N_SGU_HEADS, heads_per_group)]

    shifted_to = 0
    for sub_pieces in pieces:
        t0 = sub_pieces[0][0]
        shift_end = sub_pieces[-1][1] + sub_pieces[-1][2] - SUBLANES
        for g in range(0, d_conv, LANES):
            cols = slice(g, g + LANES)
            for r0 in range(shifted_to, shift_end, CONV_ROW_BLOCK):
                rows = min(CONV_ROW_BLOCK, shift_end - r0)
                blk = buf_ref[pl.ds(r0, rows + SUBLANES), cols]
                for r in range(1, SUBLANES):
                    shift_ref[r - 1, pl.ds(r0, rows), cols] = pltpu.roll(blk, rows + SUBLANES - r, axis=0)[:rows]
        shifted_to = shift_end
        for t, row, n in sub_pieces:
            for b0 in range(0, n, CONV_ROW_BLOCK):
                rows = min(CONV_ROW_BLOCK, n - b0)
                parts = []
                for g in range(0, d_conv, LANES):
                    cols = slice(g, g + LANES)
                    acc = jnp.broadcast_to(cb_ref[:, cols], (rows, LANES))
                    for k in range(conv_width):
                        q, r = divmod(pad + k, SUBLANES)
                        row0 = row - CTX_ROWS + b0 + q * SUBLANES
                        src = (buf_ref[pl.ds(row0, rows), cols] if r == 0
                               else shift_ref[r - 1, pl.ds(row0, rows), cols])
                        acc = acc + src * ck_ref[pl.ds(k, 1), cols]
                    parts.append(acc)
                c = jax.nn.silu(_layer_norm(jnp.concatenate(parts, axis=1), cg_ref[...], cbeta_ref[...]))
                mix_ref[pl.ds(t + b0, rows), 0:d_conv] = c.astype(bf16)

        for n0 in range(t0, t0 + sub, SGU_CHUNK):
            for p in range(d_sgu // LANES):
                cols = slice(p * LANES, (p + 1) * LANES)
                stacked = vb_ref[pl.ds(heads_per_group * n0, heads_per_group * SGU_CHUNK), cols]
                mixed = jnp.dot(w_groups[p], stacked, preferred_element_type=f32)
                s_out = u_ref[pl.ds(n0, SGU_CHUNK), cols] * (mixed + bs_ref[:, cols])
                mix_ref[pl.ds(n0, SGU_CHUNK), d_conv + p * LANES:d_conv + (p + 1) * LANES] = s_out.astype(bf16)

        m = jnp.dot(mix_ref[pl.ds(t0, sub), :], w_out_ref[...], preferred_element_type=f32)
        x = jnp.concatenate([load_rows(x_ref, t, n) for t, _, n in sub_pieces], axis=0)
        y = _layer_norm(alpha * x + m, g_ref[...], b_ref[...])
        for i, (t, row, n) in enumerate(sub_pieces):
            y_ref[t // seg, pl.ds(t % seg, n), :] = y[i * n:(i + 1) * n]
    if carry:
        buf_ref[pl.ds(0, CTX_ROWS), :] = buf_ref[pl.ds(seg, CTX_ROWS), :]


def _sgu_layout(w_sgu, b_sgu, seq, head_dim):
    mix_len = min(seq, SGU_CHUNK)
    assert SGU_CHUNK % mix_len == 0
    reps = SGU_CHUNK // mix_len
    w_tiled = jnp.tile(w_sgu[:, :, :mix_len, :mix_len], (1, 1, reps, reps))
    bias = jnp.tile(jnp.repeat(jnp.swapaxes(b_sgu[:, :, :mix_len], 1, 2), head_dim, axis=2), (1, reps, 1))
    return w_tiled, bias


def _mixer(x, ctx, layer, w_in, conv_k, conv_b, conv_g, conv_beta, sgu_g, sgu_beta, w_tiled, bias, w_out, g, b, *,
           alpha, emit_v):
    batch, seq, d_model = x.shape
    conv_width, d_conv = conv_k.shape[1:]
    conv_ctx = conv_width - 1
    d_sgu = sgu_g.shape[2]
    d_mix = w_out.shape[1]
    head_dim = d_sgu // N_SGU_HEADS
    tile = MIXER_TOKEN_TILE
    assert d_mix == d_conv + d_sgu and LANES % head_dim == 0 and CTX_ROWS >= conv_ctx
    if seq >= tile:
        assert seq % tile == 0
        n_seg, seg = 1, tile
    else:
        assert tile % seq == 0 and batch % (tile // seq) == 0
        n_seg, seg = tile // seq, seq
    assert seg >= CTX_ROWS and seg % CONV_ROW_BLOCK == 0 and tile % SUB_TILE == 0
    assert seg % SUB_TILE == 0 or SUB_TILE % seg == 0
    mix_len = min(seq, SGU_CHUNK)
    assert SUB_TILE % SGU_CHUNK == 0 and w_tiled.shape[1:] == (N_SGU_HEADS, SGU_CHUNK, SGU_CHUNK)
    assert bias.shape[1:] == (SGU_CHUNK, d_sgu)

    grid = (batch // n_seg, seq // seg)
    rows = lambda width: pl.BlockSpec((n_seg, seg, width), lambda bi, ti: (bi, ti, 0))
    per_seq = pl.BlockSpec((n_seg, conv_ctx, d_conv), lambda bi, ti: (bi, 0, 0))
    out_specs = [rows(d_model), per_seq]
    out_shape = [jax.ShapeDtypeStruct(x.shape, f32), jax.ShapeDtypeStruct((batch, conv_ctx, d_conv), f32)]
    if emit_v:
        out_specs.append(rows(d_sgu))
        out_shape.append(jax.ShapeDtypeStruct((batch, seq, d_sgu), f32))
    buf_rows = n_seg * (CTX_ROWS + seg)
    return pl.pallas_call(
        functools.partial(_mixer_kernel, alpha=alpha, mix_len=mix_len, carry=grid[1] > 1, emit_v=emit_v),
        grid=grid,
        in_specs=[rows(d_model), per_seq, _layer_block(layer, w_in.shape[1:]), _layer_block(layer, conv_k.shape[1:]),
                  _layer_block(layer, (1, d_conv)), _layer_block(layer, (1, d_conv)), _layer_block(layer, (1, d_conv)),
                  _layer_block(layer, (1, d_sgu)), _layer_block(layer, (1, d_sgu)),
                  _layer_block(layer, w_tiled.shape[1:]), _layer_block(layer, bias.shape[1:]),
                  _layer_block(layer, w_out.shape[1:]), _layer_block(layer, (1, d_model)),
                  _layer_block(layer, (1, d_model))],
        out_specs=out_specs,
        out_shape=out_shape,
        scratch_shapes=[pltpu.VMEM((buf_rows, d_conv), f32),
                        pltpu.VMEM((SUBLANES - 1, buf_rows, d_conv), f32),
                        pltpu.VMEM((tile, d_sgu), f32),
                        pltpu.VMEM((LANES // head_dim * tile, d_sgu), bf16),
                        pltpu.VMEM((tile, d_mix), bf16)],
        compiler_params=pltpu.CompilerParams(dimension_semantics=("arbitrary", "arbitrary"),
                                             vmem_limit_bytes=VMEM_LIMIT_BYTES),
        name="mixer",
    )(x, ctx, w_in, conv_k, conv_b, conv_g, conv_beta, sgu_g, sgu_beta, w_tiled, bias, w_out, g, b)


def kernel(x_prompt, x_sample, cache_conv, w_ffn1_gate, w_ffn1_up, w_ffn1_down, ln1_g, ln1_b, w_in, conv_k, conv_b,
           conv_ln_g, conv_ln_b, sgu_ln_g, sgu_ln_b, w_sgu, b_sgu, w_out, ln2_g, ln2_b, w_ffn2_gate, w_ffn2_up,
           w_ffn2_down, ln3_g, ln3_b):
    depth = w_in.shape[0]
    alpha = float((2 * depth) ** 0.25)
    batch, seq, d_model = x_prompt.shape
    conv_ctx, d_conv = cache_conv.shape[2:]
    row = lambda p: p[:, None, :]

    wg1, wu1, wd1 = w_ffn1_gate, w_ffn1_up, w_ffn1_down
    wg2, wu2, wd2 = w_ffn2_gate, w_ffn2_up, w_ffn2_down
    w_in_b, w_out_b = w_in.astype(bf16), w_out.astype(bf16)
    ln1, ln2, ln3 = (row(ln1_g), row(ln1_b)), (row(ln2_g), row(ln2_b)), (row(ln3_g), row(ln3_b))
    conv_vecs = (row(conv_b), row(conv_ln_g), row(conv_ln_b))
    sgu_vecs = (row(sgu_ln_g), row(sgu_ln_b))

    def ffn(xp, xs, l, wg, wu, wd, ln):
        yp, ys = _ffn(xp.reshape(-1, d_model), xs.reshape(-1, d_model), l, wg, wu, wd, *ln, alpha=alpha)
        return yp.reshape(xp.shape), ys.reshape(xs.shape)

    head_dim = sgu_ln_g.shape[1] // N_SGU_HEADS
    sgu_prompt = _sgu_layout(w_sgu, b_sgu, seq, head_dim)
    sgu_sample = _sgu_layout(w_sgu, b_sgu, x_sample.shape[1], head_dim)

    def mixer(x, ctx, l, sgu_layout, emit_v):
        return _mixer(x, ctx, l, w_in_b, conv_k, *conv_vecs, *sgu_vecs, *sgu_layout, w_out_b, *ln2,
                      alpha=alpha, emit_v=emit_v)

    zero_ctx = jnp.zeros((batch, conv_ctx, d_conv), f32)
    xp, xs = x_prompt, x_sample
    conv_p, conv_s, sgu_s = [], [], []
    for l in range(depth):
        xp, xs = ffn(xp, xs, l, wg1, wu1, wd1, ln1)
        xp, cp = mixer(xp, zero_ctx, l, sgu_prompt, emit_v=False)
        xs, cs, vs = mixer(xs, cache_conv[l], l, sgu_sample, emit_v=True)
        xp, xs = ffn(xp, xs, l, wg2, wu2, wd2, ln3)
        conv_p.append(cp)
        conv_s.append(cs)
        sgu_s.append(vs)
    return (xp, xs, jnp.stack(conv_p, axis=0), jnp.stack(conv_s, axis=0), jnp.stack(sgu_s, axis=0))
```

```python
import functools

import jax
import jax.numpy as jnp
from jax import lax
from jax.experimental import pallas as pl
from jax.experimental.pallas import tpu as pltpu

N_SGU_HEADS = 8
SGU_CHUNK = 128
LN_EPS = 1e-5

LANES = 128
SUBLANES = 8
BF16_SUBLANES = 16
VMEM_LIMIT_BYTES = 58 * 1024 * 1024

SUB_TILE = 512
FFN_TOKEN_TILE = 512
MIXER_TOKEN_TILE = 512
FFN_HIDDEN_CHUNK = 256
FFN_WEIGHT_PREP_STEPS = 16
CONV_ROW_BLOCK = 64
CTX_ROWS = 32

f32 = jnp.float32
bf16 = jnp.bfloat16


def _layer_norm(y, g, b):
    mu = jnp.mean(y, axis=-1, keepdims=True)
    d = y - mu
    var = jnp.mean(d * d, axis=-1, keepdims=True)
    return d * lax.rsqrt(var + LN_EPS) * g + b


def _gelu(x):
    return 0.5 * x * (1.0 + lax.erf(x * (2.0 ** -0.5)))


def _layer_block(layer, shape):
    zeros = (0,) * len(shape)
    return pl.BlockSpec((None,) + tuple(shape), lambda *_: (layer,) + zeros, pipeline_mode=pl.Buffered(1))


def _zero_words(v):
    u = pltpu.bitcast(v, jnp.uint32)
    return lax.shift_right_logical(lax.shift_right_logical(u, jnp.uint32(16)), jnp.uint32(16))


def _ffn_step(x_ref, o_ref, wg_ref, wu_ref, wd_ref, g_ref, b_ref, h_ref, s_ref, alpha):
    rows = s_ref.shape[0]
    d_ff = wg_ref.shape[1]
    if x_ref is None:
        o_ref[...] = _layer_norm(s_ref[...], g_ref[...], b_ref[...])
        return
    x = x_ref[...]
    xb = x.astype(bf16)
    chunks = list(range(0, d_ff, FFN_HIDDEN_CHUNK))
    spread = len(chunks) - 1
    bounds = [0] + [rows * c // spread // SUBLANES * SUBLANES for c in range(spread)] + [rows]
    for ci, c in enumerate(chunks):
        lhs = xb
        r0, r1 = bounds[ci], bounds[ci + 1]
        if o_ref is not None and r1 > r0:
            y = _layer_norm(s_ref[pl.ds(r0, r1 - r0), :], g_ref[...], b_ref[...])
            o_ref[pl.ds(r0, r1 - r0), :] = y
            z = _zero_words(y)
            acc = z[0:SUBLANES, 0:LANES]
            for i in range(0, r1 - r0, SUBLANES):
                for j in range(0, z.shape[1], LANES):
                    if i or j:
                        acc = acc | z[i:i + SUBLANES, j:j + LANES]
            head = pltpu.bitcast(pltpu.bitcast(x[0:SUBLANES, 0:LANES], jnp.uint32) | acc, f32)
            x_tied = jnp.concatenate([jnp.concatenate([head, x[0:SUBLANES, LANES:]], axis=1), x[SUBLANES:]], axis=0)
            lhs = x_tied.astype(bf16)
        gate = jnp.dot(lhs, wg_ref[:, c:c + FFN_HIDDEN_CHUNK], preferred_element_type=f32)
        up = jnp.dot(xb, wu_ref[:, c:c + FFN_HIDDEN_CHUNK], preferred_element_type=f32)
        h_ref[:, c:c + FFN_HIDDEN_CHUNK] = (jax.nn.silu(gate) * up).astype(bf16)
    f = jnp.dot(h_ref[...], wd_ref[...], preferred_element_type=f32)
    s_ref[...] = alpha * x + 0.5 * f


def _ffn_kernel(xp_ref, xs_ref, wg32_ref, wu32_ref, wd32_ref, g_ref, b_ref, op_ref, os_ref, wg_ref, wu_ref, wd_ref,
                h_ref, s_ref, *, alpha, prep_steps, prompt_tiles, sample_tiles):
    i = pl.program_id(0)
    j = i - prep_steps
    last = prompt_tiles + sample_tiles
    step = functools.partial(_ffn_step, wg_ref=wg_ref, wu_ref=wu_ref, wd_ref=wd_ref, g_ref=g_ref, b_ref=b_ref,
                             h_ref=h_ref, s_ref=s_ref, alpha=alpha)

    @pl.when(i < prep_steps)
    def _():
        for src, dst in ((wg32_ref, wg_ref), (wu32_ref, wu_ref), (wd32_ref, wd_ref)):
            rows = src.shape[0]
            dst[pl.ds(pl.multiple_of(i * rows, rows), rows), :] = src[...].astype(bf16)

    @pl.when(j == -1)
    def _():
        s_ref[...] = jnp.zeros(s_ref.shape, f32)

    @pl.when((j >= 0) & (j < prompt_tiles))
    def _():
        step(xp_ref, op_ref)

    @pl.when(j == prompt_tiles)
    def _():
        step(None, op_ref)

    @pl.when((j > prompt_tiles) & (j <= last))
    def _():
        step(xs_ref, os_ref)

    @pl.when(j == last + 1)
    def _():
        step(None, os_ref)


def _ffn(xp, xs, layer, wg, wu, wd, g, b, *, alpha):
    d_model = xp.shape[1]
    d_ff = wg.shape[2]
    tile = FFN_TOKEN_TILE
    prep = FFN_WEIGHT_PREP_STEPS
    assert xp.shape[0] % tile == 0 and xs.shape[0] % tile == 0 and d_ff % FFN_HIDDEN_CHUNK == 0
    assert d_model % (prep * BF16_SUBLANES) == 0 and d_ff % (prep * BF16_SUBLANES) == 0
    prompt_tiles, sample_tiles = xp.shape[0] // tile, xs.shape[0] // tile
    rows_of = lambda first, count: pl.BlockSpec((tile, d_model), lambda i: (jnp.clip(i - first, 0, count - 1), 0))
    slab = lambda rows, cols: pl.BlockSpec((None, rows // prep, cols), lambda i: (layer, jnp.minimum(i, prep - 1), 0))
    return pl.pallas_call(
        functools.partial(_ffn_kernel, alpha=alpha, prep_steps=prep, prompt_tiles=prompt_tiles,
                          sample_tiles=sample_tiles),
        grid=(prep + prompt_tiles + sample_tiles + 2,),
        in_specs=[rows_of(prep, prompt_tiles), rows_of(prep + prompt_tiles + 1, sample_tiles),
                  slab(d_model, d_ff), slab(d_model, d_ff), slab(d_ff, d_model),
                  _layer_block(layer, (1, d_model)), _layer_block(layer, (1, d_model))],
        out_specs=[rows_of(prep + 1, prompt_tiles), rows_of(prep + prompt_tiles + 2, sample_tiles)],
        out_shape=[jax.ShapeDtypeStruct(xp.shape, f32), jax.ShapeDtypeStruct(xs.shape, f32)],
        scratch_shapes=[pltpu.VMEM((d_model, d_ff), bf16), pltpu.VMEM((d_model, d_ff), bf16),
                        pltpu.VMEM((d_ff, d_model), bf16), pltpu.VMEM((tile, d_ff), bf16),
                        pltpu.VMEM((tile, d_model), f32)],
        compiler_params=pltpu.CompilerParams(dimension_semantics=("arbitrary",),
                                             vmem_limit_bytes=VMEM_LIMIT_BYTES),
        name="ffn",
    )(xp, xs, wg, wu, wd, g, b)


def _mixer_kernel(x_ref, ctx_ref, w_in_ref, ck_ref, cb_ref, cg_ref, cbeta_ref, sg_ref, sbeta_ref, ws_ref, bs_ref,
                  w_out_ref, g_ref, b_ref, *rest, alpha, mix_len, carry, emit_v):
    if emit_v:
        y_ref, state_ref, v_ref, buf_ref, shift_ref, u_ref, vb_ref, mix_ref = rest
    else:
        y_ref, state_ref, buf_ref, shift_ref, u_ref, vb_ref, mix_ref = rest
    n_seg, seg, d_model = x_ref.shape
    tile = n_seg * seg
    conv_width, d_conv = ck_ref.shape
    d_sgu = sg_ref.shape[1]
    conv_ctx = conv_width - 1
    pad = CTX_ROWS - conv_ctx
    stride = CTX_ROWS + seg
    sub = min(SUB_TILE, tile)
    piece_rows = min(sub, seg)
    pieces = [[(t, (t // seg) * stride + CTX_ROWS + t % seg, piece_rows) for t in range(t0, t0 + sub, piece_rows)]
              for t0 in range(0, tile, sub)]

    def load_rows(ref, t, rows):
        return ref[t // seg, pl.ds(t % seg, rows), :]

    head_dim = d_sgu // N_SGU_HEADS
    heads_per_group = LANES // head_dim
    lane_head = (lax.broadcasted_iota(jnp.int32, (SGU_CHUNK, d_sgu), 1) // head_dim) % heads_per_group

    @pl.when(pl.program_id(1) == 0)
    def _():
        for s in range(n_seg):
            buf_ref[pl.ds(s * stride, SUBLANES), :] = jnp.zeros((SUBLANES, d_conv), f32)
            buf_ref[pl.ds(s * stride + pad, conv_ctx), :] = ctx_ref[s]

    for sub_pieces in pieces:
        t0 = sub_pieces[0][0]
        xb = jnp.concatenate([load_rows(x_ref, t, n) for t, _, n in sub_pieces], axis=0).astype(bf16)
        a_val = jnp.dot(xb, w_in_ref[:, 0:d_conv], preferred_element_type=f32)
        a_gate = jnp.dot(xb, w_in_ref[:, d_conv:2 * d_conv], preferred_element_type=f32)
        a = a_val * jax.nn.sigmoid(a_gate)
        for i, (t, row, n) in enumerate(sub_pieces):
            buf_ref[pl.ds(row, n), :] = a[i * n:(i + 1) * n]
        z_u = jnp.dot(xb, w_in_ref[:, 2 * d_conv:2 * d_conv + d_sgu], preferred_element_type=f32)
        u_ref[pl.ds(t0, sub), :] = _gelu(z_u)
        z_v = jnp.dot(xb, w_in_ref[:, 2 * d_conv + d_sgu:], preferred_element_type=f32)
        v = _layer_norm(_gelu(z_v), sg_ref[...], sbeta_ref[...])
        for c0 in range(0, sub, SGU_CHUNK):
            for j in range(heads_per_group):
                masked = jnp.where(lane_head == j, v[c0:c0 + SGU_CHUNK], 0.0)
                vb_ref[pl.ds(heads_per_group * (t0 + c0) + j * SGU_CHUNK, SGU_CHUNK), :] = masked.astype(bf16)
        if emit_v:
            for i, (t, row, n) in enumerate(sub_pieces):
                v_ref[t // seg, pl.ds(t % seg, n), :] = v[i * n:(i + 1) * n]
    for s in range(n_seg):
        state_ref[s] = buf_ref[pl.ds((s + 1) * stride - conv_ctx, conv_ctx), :]

    t_idx = lax.broadcasted_iota(jnp.int32, (SGU_CHUNK, SGU_CHUNK), 0)
    s_idx = lax.broadcasted_iota(jnp.int32, (SGU_CHUNK, SGU_CHUNK), 1)
    allowed = (t_idx // mix_len == s_idx // mix_len) & (t_idx >= s_idx)
    w_heads = [jnp.where(allowed, ws_ref[h], 0.0).astype(bf16) for h in range(N_SGU_HEADS)]
    w_groups = [jnp.concatenate(w_heads[p:p + heads_per_group], axis=1)
                for p in range(0, N_SGU_HEADS, heads_per_group)]

    shifted_to = 0
    for sub_pieces in pieces:
        t0 = sub_pieces[0][0]
        shift_end = sub_pieces[-1][1] + sub_pieces[-1][2] - SUBLANES
        for g in range(0, d_conv, LANES):
            cols = slice(g, g + LANES)
            for r0 in range(shifted_to, shift_end, CONV_ROW_BLOCK):
                rows = min(CONV_ROW_BLOCK, shift_end - r0)
                blk = buf_ref[pl.ds(r0, rows + SUBLANES), cols]
                for r in range(1, SUBLANES):
                    shift_ref[r - 1, pl.ds(r0, rows), cols] = pltpu.roll(blk, rows + SUBLANES - r, axis=0)[:rows]
        shifted_to = shift_end
        for t, row, n in sub_pieces:
            for b0 in range(0, n, CONV_ROW_BLOCK):
                rows = min(CONV_ROW_BLOCK, n - b0)
                parts = []
                for g in range(0, d_conv, LANES):
                    cols = slice(g, g + LANES)
                    acc = jnp.broadcast_to(cb_ref[:, cols], (rows, LANES))
                    for k in range(conv_width):
                        q, r = divmod(pad + k, SUBLANES)
                        row0 = row - CTX_ROWS + b0 + q * SUBLANES
                        src = (buf_ref[pl.ds(row0, rows), cols] if r == 0
                               else shift_ref[r - 1, pl.ds(row0, rows), cols])
                        acc = acc + src * ck_ref[pl.ds(k, 1), cols]
                    parts.append(acc)
                c = jax.nn.silu(_layer_norm(jnp.concatenate(parts, axis=1), cg_ref[...], cbeta_ref[...]))
                mix_ref[pl.ds(t + b0, rows), 0:d_conv] = c.astype(bf16)

        for n0 in range(t0, t0 + sub, SGU_CHUNK):
            for p in range(d_sgu // LANES):
                cols = slice(p * LANES, (p + 1) * LANES)
                stacked = vb_ref[pl.ds(heads_per_group * n0, heads_per_group * SGU_CHUNK), cols]
                mixed = jnp.dot(w_groups[p], stacked, preferred_element_type=f32)
                s_out = u_ref[pl.ds(n0, SGU_CHUNK), cols] * (mixed + bs_ref[:, cols])
                mix_ref[pl.ds(n0, SGU_CHUNK), d_conv + p * LANES:d_conv + (p + 1) * LANES] = s_out.astype(bf16)

        m = jnp.dot(mix_ref[pl.ds(t0, sub), :], w_out_ref[...], preferred_element_type=f32)
        x = jnp.concatenate([load_rows(x_ref, t, n) for t, _, n in sub_pieces], axis=0)
        y = _layer_norm(alpha * x + m, g_ref[...], b_ref[...])
        for i, (t, row, n) in enumerate(sub_pieces):
            y_ref[t // seg, pl.ds(t % seg, n), :] = y[i * n:(i + 1) * n]
    if carry:
        buf_ref[pl.ds(0, CTX_ROWS), :] = buf_ref[pl.ds(seg, CTX_ROWS), :]


def _sgu_layout(w_sgu, b_sgu, seq, head_dim):
    mix_len = min(seq, SGU_CHUNK)
    assert SGU_CHUNK % mix_len == 0
    reps = SGU_CHUNK // mix_len
    w_tiled = jnp.tile(w_sgu[:, :, :mix_len, :mix_len], (1, 1, reps, reps))
    bias = jnp.tile(jnp.repeat(jnp.swapaxes(b_sgu[:, :, :mix_len], 1, 2), head_dim, axis=2), (1, reps, 1))
    return w_tiled, bias


def _mixer(x, ctx, layer, w_in, conv_k, conv_b, conv_g, conv_beta, sgu_g, sgu_beta, w_tiled, bias, w_out, g, b, *,
           alpha, emit_v):
    batch, seq, d_model = x.shape
    conv_width, d_conv = conv_k.shape[1:]
    conv_ctx = conv_width - 1
    d_sgu = sgu_g.shape[2]
    d_mix = w_out.shape[1]
    head_dim = d_sgu // N_SGU_HEADS
    tile = MIXER_TOKEN_TILE
    assert d_mix == d_conv + d_sgu and LANES % head_dim == 0 and CTX_ROWS >= conv_ctx
    if seq >= tile:
        assert seq % tile == 0
        n_seg, seg = 1, tile
    else:
        assert tile % seq == 0 and batch % (tile // seq) == 0
        n_seg, seg = tile // seq, seq
    assert seg >= CTX_ROWS and seg % CONV_ROW_BLOCK == 0 and tile % SUB_TILE == 0
    assert seg % SUB_TILE == 0 or SUB_TILE % seg == 0
    mix_len = min(seq, SGU_CHUNK)
    assert SUB_TILE % SGU_CHUNK == 0 and w_tiled.shape[1:] == (N_SGU_HEADS, SGU_CHUNK, SGU_CHUNK)
    assert bias.shape[1:] == (SGU_CHUNK, d_sgu)

    grid = (batch // n_seg, seq // seg)
    rows = lambda width: pl.BlockSpec((n_seg, seg, width), lambda bi, ti: (bi, ti, 0))
    per_seq = pl.BlockSpec((n_seg, conv_ctx, d_conv), lambda bi, ti: (bi, 0, 0))
    out_specs = [rows(d_model), per_seq]
    out_shape = [jax.ShapeDtypeStruct(x.shape, f32), jax.ShapeDtypeStruct((batch, conv_ctx, d_conv), f32)]
    if emit_v:
        out_specs.append(rows(d_sgu))
        out_shape.append(jax.ShapeDtypeStruct((batch, seq, d_sgu), f32))
    buf_rows = n_seg * (CTX_ROWS + seg)
    return pl.pallas_call(
        functools.partial(_mixer_kernel, alpha=alpha, mix_len=mix_len, carry=grid[1] > 1, emit_v=emit_v),
        grid=grid,
        in_specs=[rows(d_model), per_seq, _layer_block(layer, w_in.shape[1:]), _layer_block(layer, conv_k.shape[1:]),
                  _layer_block(layer, (1, d_conv)), _layer_block(layer, (1, d_conv)), _layer_block(layer, (1, d_conv)),
                  _layer_block(layer, (1, d_sgu)), _layer_block(layer, (1, d_sgu)),
                  _layer_block(layer, w_tiled.shape[1:]), _layer_block(layer, bias.shape[1:]),
                  _layer_block(layer, w_out.shape[1:]), _layer_block(layer, (1, d_model)),
                  _layer_block(layer, (1, d_model))],
        out_specs=out_specs,
        out_shape=out_shape,
        scratch_shapes=[pltpu.VMEM((buf_rows, d_conv), f32),
                        pltpu.VMEM((SUBLANES - 1, buf_rows, d_conv), f32),
                        pltpu.VMEM((tile, d_sgu), f32),
                        pltpu.VMEM((LANES // head_dim * tile, d_sgu), bf16),
                        pltpu.VMEM((tile, d_mix), bf16)],
        compiler_params=pltpu.CompilerParams(dimension_semantics=("arbitrary", "arbitrary"),
                                             vmem_limit_bytes=VMEM_LIMIT_BYTES),
        name="mixer",
    )(x, ctx, w_in, conv_k, conv_b, conv_g, conv_beta, sgu_g, sgu_beta, w_tiled, bias, w_out, g, b)


def kernel(x_prompt, x_sample, cache_conv, w_ffn1_gate, w_ffn1_up, w_ffn1_down, ln1_g, ln1_b, w_in, conv_k, conv_b,
           conv_ln_g, conv_ln_b, sgu_ln_g, sgu_ln_b, w_sgu, b_sgu, w_out, ln2_g, ln2_b, w_ffn2_gate, w_ffn2_up,
           w_ffn2_down, ln3_g, ln3_b):
    depth = w_in.shape[0]
    alpha = float((2 * depth) ** 0.25)
    batch, seq, d_model = x_prompt.shape
    conv_ctx, d_conv = cache_conv.shape[2:]
    row = lambda p: p[:, None, :]

    wg1, wu1, wd1 = w_ffn1_gate, w_ffn1_up, w_ffn1_down
    wg2, wu2, wd2 = w_ffn2_gate, w_ffn2_up, w_ffn2_down
    w_in_b, w_out_b = w_in.astype(bf16), w_out.astype(bf16)
    ln1, ln2, ln3 = (row(ln1_g), row(ln1_b)), (row(ln2_g), row(ln2_b)), (row(ln3_g), row(ln3_b))
    conv_vecs = (row(conv_b), row(conv_ln_g), row(conv_ln_b))
    sgu_vecs = (row(sgu_ln_g), row(sgu_ln_b))

    def ffn(xp, xs, l, wg, wu, wd, ln):
        yp, ys = _ffn(xp.reshape(-1, d_model), xs.reshape(-1, d_model), l, wg, wu, wd, *ln, alpha=alpha)
        return yp.reshape(xp.shape), ys.reshape(xs.shape)

    head_dim = sgu_ln_g.shape[1] // N_SGU_HEADS
    sgu_prompt = _sgu_layout(w_sgu, b_sgu, seq, head_dim)
    sgu_sample = _sgu_layout(w_sgu, b_sgu, x_sample.shape[1], head_dim)

    def mixer(x, ctx, l, sgu_layout, emit_v):
        return _mixer(x, ctx, l, w_in_b, conv_k, *conv_vecs, *sgu_vecs, *sgu_layout, w_out_b, *ln2,
                      alpha=alpha, emit_v=emit_v)

    zero_ctx = jnp.zeros((batch, conv_ctx, d_conv), f32)
    xp, xs = x_prompt, x_sample
    conv_p, conv_s, sgu_s = [], [], []
    for l in range(depth):
        xp, xs = ffn(xp, xs, l, wg1, wu1, wd1, ln1)
        xp, cp = mixer(xp, zero_ctx, l, sgu_prompt, emit_v=False)
        xs, cs, vs = mixer(xs, cache_conv[l], l, sgu_sample, emit_v=True)
        xp, xs = ffn(xp, xs, l, wg2, wu2, wd2, ln3)
        conv_p.append(cp)
        conv_s.append(cs)
        sgu_s.append(vs)
    return (xp, xs, jnp.stack(conv_p, axis=0), jnp.stack(conv_s, axis=0), jnp.stack(sgu_s, axis=0))
```
